```python
import jax, jax.numpy as jnp
from jax import lax
import numpy as np

D_MODEL = 2048
BATCH = 4
SEQ = 2048
DEPTH = 2

N_MIXERS = 2
N_MOBA_LAYERS = (DEPTH + 1) // 2
N_RET_LAYERS = DEPTH // 2

FFN_DIM = 5632
FFN_RES = 0.5
RMS_EPS = 1e-6

MOBA_HEADS = 16
MOBA_HEAD_DIM = D_MODEL // MOBA_HEADS
MOBA_BLOCK = 256
MOBA_TOPK = 3
MOBA_Q_CHUNK = 16
ROPE_THETA = 500000.0
ROPE_DIM = MOBA_HEAD_DIM // 4

RET_HEADS = 8
RET_KEY_DIM = D_MODEL // RET_HEADS
RET_VAL_DIM = D_MODEL // RET_HEADS
RET_CHUNK = 128
RET_ROT_BASE = 10000.0

kernel_name = "hybrid_moba_retention_macaron"


def rms_norm(x, g):
    xf = x.astype(jnp.float32)
    y = xf * lax.rsqrt(jnp.mean(xf * xf, axis=-1, keepdims=True) + RMS_EPS)
    return (y * g.astype(jnp.float32)).astype(x.dtype)


def swiglu(x, w_gate_up, w_down):
    g, u = jnp.split(x @ w_gate_up, 2, axis=-1)
    return (jax.nn.silu(g) * u) @ w_down


def rotary(x, pos, rot_dim, inv_freq):
    half = rot_dim // 2
    ang = pos.astype(jnp.float32)[:, None] * inv_freq[None, :]
    cos = jnp.cos(ang).astype(x.dtype)
    sin = jnp.sin(ang).astype(x.dtype)
    x1 = x[..., :half]
    x2 = x[..., half:rot_dim]
    return jnp.concatenate([x1 * cos - x2 * sin, x1 * sin + x2 * cos, x[..., rot_dim:]], axis=-1)


def moba_attention(h, w_qkv, w_o):
    B, S, _ = h.shape
    H, Dh, BLK, QC = MOBA_HEADS, MOBA_HEAD_DIM, MOBA_BLOCK, MOBA_Q_CHUNK
    qkv = (h @ w_qkv).reshape(B, S, 3, H, Dh)
    q = jnp.transpose(qkv[:, :, 0], (0, 2, 1, 3))
    k = jnp.transpose(qkv[:, :, 1], (0, 2, 1, 3))
    v = jnp.transpose(qkv[:, :, 2], (0, 2, 1, 3))
    pos = jnp.arange(S)
    half = ROPE_DIM // 2
    inv_freq = jnp.power(jnp.float32(ROPE_THETA), -jnp.arange(half, dtype=jnp.float32) / half)
    q = rotary(q, pos, ROPE_DIM, inv_freq)
    k = rotary(k, pos, ROPE_DIM, inv_freq)

    n_blk = -(-S // BLK)
    pad = n_blk * BLK - S
    kp = jnp.pad(k, ((0, 0), (0, 0), (0, pad), (0, 0)))
    vp = jnp.pad(v, ((0, 0), (0, 0), (0, pad), (0, 0)))
    k_blocks = kp.reshape(B, H, n_blk, BLK, Dh)
    v_blocks = vp.reshape(B, H, n_blk, BLK, Dh)

    k_mean = jnp.mean(k_blocks.astype(jnp.float32), axis=3)
    gate = jnp.einsum('bhsd,bhnd->bhsn', q.astype(jnp.float32), k_mean)
    q_blk = pos // BLK
    past = jnp.arange(n_blk)[None, :] < q_blk[:, None]
    gate = jnp.where(past[None, None], gate, -jnp.inf)
    k_sel = min(MOBA_TOPK, n_blk)
    _, sel = lax.top_k(gate, k_sel)
    sel_valid = sel < q_blk[None, None, :, None]

    n_qc = S // QC
    q_c_all = q.reshape(B, H, n_qc, QC, Dh).transpose(2, 0, 1, 3, 4)
    sel_all = sel.reshape(B, H, n_qc, QC, k_sel).transpose(2, 0, 1, 3, 4)
    val_all = sel_valid.reshape(B, H, n_qc, QC, k_sel).transpose(2, 0, 1, 3, 4)
    bi = jnp.arange(B)[:, None, None, None]
    hi = jnp.arange(H)[None, :, None, None]
    scale = Dh ** -0.5

    def chunk(args):
        c, q_c, sel_c, val_c = args
        q_start = c * QC
        own = q_start // BLK
        k_own = lax.dynamic_slice_in_dim(kp, own * BLK, BLK, axis=2)
        v_own = lax.dynamic_slice_in_dim(vp, own * BLK, BLK, axis=2)
        k_g = k_blocks[bi, hi, sel_c]
        v_g = v_blocks[bi, hi, sel_c]
        s_sel = jnp.einsum('bhqd,bhqjkd->bhqjk', q_c, k_g).astype(jnp.float32) * scale
        s_sel = jnp.where(val_c[..., None], s_sel, -jnp.inf).reshape(B, H, QC, k_sel * BLK)
        s_own = jnp.einsum('bhqd,bhkd->bhqk', q_c, k_own).astype(jnp.float32) * scale
        qpos = q_start + jnp.arange(QC)
        kpos = own * BLK + jnp.arange(BLK)
        s_own = jnp.where(kpos[None, :] <= qpos[:, None], s_own, -jnp.inf)
        p = jax.nn.softmax(jnp.concatenate([s_sel, s_own], axis=-1), axis=-1).astype(v.dtype)
        p_sel = p[..., :k_sel * BLK].reshape(B, H, QC, k_sel, BLK)
        p_own = p[..., k_sel * BLK:]
        return (jnp.einsum('bhqjk,bhqjkd->bhqd', p_sel, v_g)
                + jnp.einsum('bhqk,bhkd->bhqd', p_own, v_own))

    out = lax.map(chunk, (jnp.arange(n_qc), q_c_all, sel_all, val_all))
    out = out.transpose(1, 0, 3, 2, 4).reshape(B, S, H * Dh)
    return out @ w_o


def retention(h, w_in, w_o, gn_gain):
    B, S, _ = h.shape
    H, dk, dv, C = RET_HEADS, RET_KEY_DIM, RET_VAL_DIM, RET_CHUNK
    proj = h @ w_in
    q, k, v, g = jnp.split(proj, [H * dk, 2 * H * dk, 2 * H * dk + H * dv], axis=-1)
    q = q.reshape(B, S, H, dk).transpose(0, 2, 1, 3).astype(jnp.float32)
    k = k.reshape(B, S, H, dk).transpose(0, 2, 1, 3).astype(jnp.float32)
    v = v.reshape(B, S, H, dv).transpose(0, 2, 1, 3).astype(jnp.float32)
    pos = jnp.arange(S)
    inv_freq = jnp.power(jnp.float32(RET_ROT_BASE), -jnp.linspace(0.0, 1.0, dk // 2, dtype=jnp.float32))
    q = rotary(q, pos, dk, inv_freq)
    k = rotary(k, pos, dk, inv_freq) * (dk ** -0.5)

    log_gamma = jnp.log1p(-jnp.power(2.0, -5.0 - jnp.arange(H, dtype=jnp.float32)))
    n = jnp.arange(C, dtype=jnp.float32)
    diff = n[:, None] - n[None, :]
    tri = diff >= 0
    decay_mask = jnp.exp(jnp.where(tri[None], diff[None] * log_gamma[:, None, None], -jnp.inf))
    q_decay = jnp.exp((n[None, :] + 1.0) * log_gamma[:, None])
    k_decay = jnp.exp((C - 1.0 - n[None, :]) * log_gamma[:, None])
    chunk_decay = jnp.exp(C * log_gamma)

    nC = S // C
    def to_chunks(t):
        return t.reshape(B, H, nC, C, t.shape[-1]).transpose(2, 0, 1, 3, 4)

    def step(state, inp):
        qc, kc, vc = inp
        s = jnp.einsum('bhnd,bhmd->bhnm', qc, kc) * decay_mask[None]
        inner = jnp.einsum('bhnm,bhmv->bhnv', s, vc)
        cross = jnp.einsum('bhnd,bhdv->bhnv', qc, state) * q_decay[None, :, :, None]
        new_state = (state * chunk_decay[None, :, None, None]
                     + jnp.einsum('bhmd,bhmv->bhdv', kc * k_decay[None, :, :, None], vc))
        return new_state, inner + cross

    state0 = jnp.zeros((B, H, dk, dv), jnp.float32)
    _, out = lax.scan(step, state0, (to_chunks(q), to_chunks(k), to_chunks(v)))
    out = out.transpose(1, 0, 3, 2, 4).reshape(B, S, H, dv)
    out = out * lax.rsqrt(jnp.mean(out * out, axis=-1, keepdims=True) + RMS_EPS)
    out = out.reshape(B, S, H * dv) * gn_gain.astype(jnp.float32)
    y = (jax.nn.silu(g.astype(jnp.float32)) * out).astype(h.dtype)
    return y @ w_o


def setup_inputs(seed: int = 0) -> dict:
    key = jax.random.key(seed)
    ks = jax.random.split(key, 10)

    def w(k, shape, fan_in):
        return jax.random.normal(k, shape, jnp.float32) * (fan_in ** -0.5)

    x = jax.random.normal(ks[0], (BATCH, SEQ, D_MODEL), jnp.float32)
    norm_gain = 1.0 + 0.02 * jax.random.normal(ks[1], (DEPTH, 3, D_MODEL), jnp.float32)
    ffn_w_gate_up = w(ks[2], (DEPTH, 2, D_MODEL, 2 * FFN_DIM), D_MODEL)
    ffn_w_down = w(ks[3], (DEPTH, 2, FFN_DIM, D_MODEL), FFN_DIM)
    moba_w_qkv = w(ks[4], (N_MOBA_LAYERS, D_MODEL, 3 * MOBA_HEADS * MOBA_HEAD_DIM), D_MODEL)
    moba_w_o = w(ks[5], (N_MOBA_LAYERS, MOBA_HEADS * MOBA_HEAD_DIM, D_MODEL), MOBA_HEADS * MOBA_HEAD_DIM)
    ret_w_in = w(ks[6], (N_RET_LAYERS, D_MODEL, 2 * RET_HEADS * RET_KEY_DIM + 2 * RET_HEADS * RET_VAL_DIM), D_MODEL)
    ret_w_o = w(ks[7], (N_RET_LAYERS, RET_HEADS * RET_VAL_DIM, D_MODEL), RET_HEADS * RET_VAL_DIM)
    ret_gn_gain = 1.0 + 0.02 * jax.random.normal(ks[8], (N_RET_LAYERS, RET_HEADS * RET_VAL_DIM), jnp.float32)
    final_norm = 1.0 + 0.02 * jax.random.normal(ks[9], (D_MODEL,), jnp.float32)
    return {"x": x, "norm_gain": norm_gain, "ffn_w_gate_up": ffn_w_gate_up, "ffn_w_down": ffn_w_down,
            "moba_w_qkv": moba_w_qkv, "moba_w_o": moba_w_o, "ret_w_in": ret_w_in, "ret_w_o": ret_w_o,
            "ret_gn_gain": ret_gn_gain, "final_norm": final_norm}


def reference(x, norm_gain, ffn_w_gate_up, ffn_w_down, moba_w_qkv, moba_w_o,
              ret_w_in, ret_w_o, ret_gn_gain, final_norm):
    for i in range(DEPTH):
        g = norm_gain[i]
        x = x + FFN_RES * swiglu(rms_norm(x, g[0]), ffn_w_gate_up[i, 0], ffn_w_down[i, 0])
        h = rms_norm(x, g[1])
        j = i // N_MIXERS
        if i % N_MIXERS == 0:
            x = x + moba_attention(h, moba_w_qkv[j], moba_w_o[j])
        else:
            x = x + retention(h, ret_w_in[j], ret_w_o[j], ret_gn_gain[j])
        x = x + FFN_RES * swiglu(rms_norm(x, g[2]), ffn_w_gate_up[i, 1], ffn_w_down[i, 1])
    return rms_norm(x, final_norm)
```

```python
import functools

import jax
import jax.numpy as jnp
from jax import lax
from jax.experimental import pallas as pl
from jax.experimental.pallas import tpu as pltpu

FFN_RES = 0.5
RMS_EPS = 1e-6

MOBA_HEAD_DIM = 128
MOBA_BLOCK = 256
MOBA_TOPK = 3
ROPE_THETA = 500000.0
ROPE_DIM = MOBA_HEAD_DIM // 4

RET_HEAD_DIM = 256
RET_CHUNK = 128
RET_ROT_BASE = 10000.0

V7X_VMEM_BYTES = 64 * 1024 * 1024
V7X_LANES = 128
VMEM_LIMIT_BYTES = V7X_VMEM_BYTES - 8 * 1024 * 1024

F32 = jnp.float32
BF16 = jnp.bfloat16


def _params(*semantics):
    return pltpu.CompilerParams(dimension_semantics=semantics, vmem_limit_bytes=VMEM_LIMIT_BYTES)


def _rms_norm(x, gain):
    return x * lax.rsqrt(jnp.mean(x * x, axis=-1, keepdims=True) + RMS_EPS) * gain


def _dot(a, b):
    return jnp.dot(a, b, preferred_element_type=F32)


def _dot_nt(a, b):
    return lax.dot_general(a, b, (((1,), (1,)), ((), ())), preferred_element_type=F32)


def _ffn_kernel(x_ref, gain_ref, wg_ref, wu_ref, wd_ref, fgain_ref, o_ref, xn_ref, acc_ref, *, final_norm):
    j = pl.program_id(1)

    @pl.when(j == 0)
    def _():
        xn_ref[...] = _rms_norm(x_ref[...], gain_ref[...]).astype(BF16)
        acc_ref[...] = jnp.zeros_like(acc_ref)

    xn = xn_ref[...]
    g = _dot(xn, wg_ref[...])
    u = _dot(xn, wu_ref[...])
    h = (g * jax.nn.sigmoid(g) * u).astype(BF16)
    acc_ref[...] += _dot(h, wd_ref[...])

    @pl.when(j == pl.num_programs(1) - 1)
    def _():
        y = x_ref[...] + FFN_RES * acc_ref[...]
        if final_norm:
            y = _rms_norm(y, fgain_ref[...])
        o_ref[...] = y


def _ffn(x, gain, w_gate_up, w_down, fgain, *, final_norm, tm=512, tf=512):
    t, d = x.shape
    f = w_down.shape[0]
    nf = f // tf
    return pl.pallas_call(
        functools.partial(_ffn_kernel, final_norm=final_norm),
        grid=(t // tm, nf),
        in_specs=[
            pl.BlockSpec((tm, d), lambda i, j: (i, 0)),
            pl.BlockSpec((1, d), lambda i, j: (0, 0)),
            pl.BlockSpec((d, tf), lambda i, j: (0, j)),
            pl.BlockSpec((d, tf), lambda i, j: (0, j + nf)),
            pl.BlockSpec((tf, d), lambda i, j: (j, 0)),
            pl.BlockSpec((1, d), lambda i, j: (0, 0)),
        ],
        out_specs=pl.BlockSpec((tm, d), lambda i, j: (i, 0)),
        out_shape=jax.ShapeDtypeStruct((t, d), F32),
        scratch_shapes=[pltpu.VMEM((tm, d), BF16), pltpu.VMEM((tm, d), F32)],
        compiler_params=_params("parallel", "arbitrary"),
        name="ffn",
    )(x, gain.reshape(1, d), w_gate_up, w_gate_up, w_down, fgain.reshape(1, d))


def _out_proj_kernel(a_ref, w_ref, x_ref, o_ref):
    o_ref[...] = x_ref[...] + _dot(a_ref[...], w_ref[...])


def _out_proj(a, w, x, *, tm=1024, tn=1024):
    t, k = a.shape
    n = w.shape[1]
    tn = min(tn, n)
    return pl.pallas_call(
        _out_proj_kernel,
        grid=(t // tm, n // tn),
        in_specs=[
            pl.BlockSpec((tm, k), lambda i, j: (i, 0)),
            pl.BlockSpec((k, tn), lambda i, j: (0, j)),
            pl.BlockSpec((tm, tn), lambda i, j: (i, j)),
        ],
        out_specs=pl.BlockSpec((tm, tn), lambda i, j: (i, j)),
        out_shape=jax.ShapeDtypeStruct((t, n), F32),
        compiler_params=_params("parallel", "parallel"),
        name="out_proj",
    )(a, w, x)


def _moba_rope_tables(seq):
    half = ROPE_DIM // 2
    inv_freq = jnp.power(jnp.float32(ROPE_THETA), -jnp.arange(half, dtype=F32) / half)
    ang = jnp.arange(seq).astype(F32)[:, None] * inv_freq[None, :]
    cos, sin = jnp.cos(ang), jnp.sin(ang)
    rest = MOBA_HEAD_DIM - ROPE_DIM
    c = jnp.concatenate([cos, cos, jnp.ones((seq, rest), F32)], axis=-1)
    a = jnp.concatenate([-sin, jnp.zeros((seq, half + rest), F32)], axis=-1)
    b = jnp.concatenate([jnp.zeros((seq, half), F32), sin, jnp.zeros((seq, rest), F32)], axis=-1)
    return c, a, b


def _moba_proj_kernel(x_ref, gain_ref, w_ref, c_ref, a_ref, b_ref, o_ref, xn_ref, *, n_rot_tiles):
    j = pl.program_id(1)

    @pl.when(j == 0)
    def _():
        xn_ref[...] = _rms_norm(x_ref[...], gain_ref[...]).astype(BF16)

    y = _dot(xn_ref[...], w_ref[...])

    @pl.when(j < n_rot_tiles)
    def _():
        half = ROPE_DIM // 2
        c, a, b = c_ref[...], a_ref[...], b_ref[...]
        for h in range(y.shape[1] // MOBA_HEAD_DIM):
            sl = slice(h * MOBA_HEAD_DIM, (h + 1) * MOBA_HEAD_DIM)
            yh = y[:, sl]
            up = pltpu.roll(yh, MOBA_HEAD_DIM - half, 1)
            down = pltpu.roll(yh, half, 1)
            o_ref[:, sl] = yh * c + up * a + down * b

    @pl.when(j >= n_rot_tiles)
    def _():
        o_ref[...] = y


def _moba_proj(x, gain, w_qkv, seq, *, tm=1024, tn=512):
    t, d = x.shape
    n = w_qkv.shape[1]
    c, a, b = _moba_rope_tables(seq)
    pos_tiles = seq // tm
    tab_spec = pl.BlockSpec((tm, MOBA_HEAD_DIM), lambda i, j: (i % pos_tiles, 0))
    return pl.pallas_call(
        functools.partial(_moba_proj_kernel, n_rot_tiles=(2 * d) // tn),
        grid=(t // tm, n // tn),
        in_specs=[
            pl.BlockSpec((tm, d), lambda i, j: (i, 0)),
            pl.BlockSpec((1, d), lambda i, j: (0, 0)),
            pl.BlockSpec((d, tn), lambda i, j: (0, j)),
            tab_spec, tab_spec, tab_spec,
        ],
        out_specs=pl.BlockSpec((tm, tn), lambda i, j: (i, j)),
        out_shape=jax.ShapeDtypeStruct((t, n), F32),
        scratch_shapes=[pltpu.VMEM((tm, d), BF16)],
        compiler_params=_params("parallel", "arbitrary"),
        name="moba_proj",
    )(x, gain.reshape(1, d), w_qkv, c, a, b)


def _split_bf16(x):
    hi = x.astype(BF16)
    lo = (x - hi.astype(F32)).astype(BF16)
    return hi, lo


def _moba_attn_kernel(q_ref, k_ref, v_ref, o_ref):
    seq = q_ref.shape[0]
    blk = MOBA_BLOCK
    n_blk = seq // blk
    scale = MOBA_HEAD_DIM ** -0.5

    k = k_ref[...]
    k16 = k.astype(BF16)
    vt16 = v_ref[...].T.astype(BF16)
    k_mean = jnp.mean(k.reshape(n_blk, blk, MOBA_HEAD_DIM), axis=1)
    km_hi, km_lo = _split_bf16(k_mean)

    blk_row = lax.broadcasted_iota(jnp.int32, (n_blk, blk), 0)
    key_in_blk = lax.broadcasted_iota(jnp.int32, (blk, blk), 0)
    qry_in_blk = lax.broadcasted_iota(jnp.int32, (blk, blk), 1)
    causal = key_in_blk <= qry_in_blk

    for i in range(n_blk):
        q = q_ref[i * blk:(i + 1) * blk, :]
        n_keys = (i + 1) * blk
        st = _dot_nt(k16[:n_keys], (q * scale).astype(BF16))

        if i > MOBA_TOPK:
            q_hi, q_lo = _split_bf16(q)
            gate = _dot_nt(km_hi, q_hi) + _dot_nt(km_lo, q_hi) + _dot_nt(km_hi, q_lo)
            past = blk_row < i

        masked = []
        for j in range(i + 1):
            s = st[j * blk:(j + 1) * blk]
            if j == i:
                s = jnp.where(causal, s, -jnp.inf)
            elif i > MOBA_TOPK:
                gj = gate[j:j + 1, :]
                beats = ((gate > gj) | ((gate == gj) & (blk_row < j))) & past
                rank = jnp.sum(beats.astype(F32), axis=0, keepdims=True)
                s = jnp.where(rank < MOBA_TOPK, s, -jnp.inf)
            masked.append(s)

        m = masked[0]
        for s in masked[1:]:
            m = jnp.maximum(m, s)
        m = jnp.max(m, axis=0, keepdims=True)
        p = [jnp.exp(s - m) for s in masked]
        l = p[0]
        for pj in p[1:]:
            l = l + pj
        l = jnp.sum(l, axis=0, keepdims=True)
        pt = jnp.concatenate([pj.astype(BF16) for pj in p], axis=0)
        ot = _dot(vt16[:, :n_keys], pt) / l
        o_ref[i * blk:(i + 1) * blk, :] = ot.T.astype(o_ref.dtype)


def _moba_attn(qkv, batch, seq, d):
    heads = d // MOBA_HEAD_DIM
    blk = (seq, MOBA_HEAD_DIM)
    return pl.pallas_call(
        _moba_attn_kernel,
        grid=(batch, heads),
        in_specs=[
            pl.BlockSpec(blk, lambda b, h: (b, h)),
            pl.BlockSpec(blk, lambda b, h: (b, heads + h)),
            pl.BlockSpec(blk, lambda b, h: (b, 2 * heads + h)),
        ],
        out_specs=pl.BlockSpec(blk, lambda b, h: (b, h)),
        out_shape=jax.ShapeDtypeStruct((batch * seq, d), BF16),
        compiler_params=_params("parallel", "parallel"),
        name="moba_attn",
    )(qkv, qkv, qkv)


def _ret_proj_kernel(x_ref, gain_ref, w_ref, cos_ref, sin_ref, o_ref, xn_ref, *, n_q_tiles):
    j = pl.program_id(1)

    @pl.when(j == 0)
    def _():
        xn_ref[...] = _rms_norm(x_ref[...], gain_ref[...]).astype(BF16)

    y = _dot(xn_ref[...], w_ref[...])
    half = RET_HEAD_DIM // 2

    def rotate(scale):
        cos, sin = cos_ref[...], sin_ref[...]
        for h in range(y.shape[1] // RET_HEAD_DIM):
            lo = h * RET_HEAD_DIM
            x1, x2 = y[:, lo:lo + half], y[:, lo + half:lo + RET_HEAD_DIM]
            o_ref[:, lo:lo + half] = (x1 * cos - x2 * sin) * scale
            o_ref[:, lo + half:lo + RET_HEAD_DIM] = (x1 * sin + x2 * cos) * scale

    @pl.when(j < n_q_tiles)
    def _():
        rotate(1.0)

    @pl.when((j >= n_q_tiles) & (j < 2 * n_q_tiles))
    def _():
        rotate(RET_HEAD_DIM ** -0.5)

    @pl.when(j >= 2 * n_q_tiles)
    def _():
        o_ref[...] = y


def _ret_proj(x, gain, w_in, seq, *, tm=1024, tn=512):
    t, d = x.shape
    n = w_in.shape[1]
    half = RET_HEAD_DIM // 2
    inv_freq = jnp.power(jnp.float32(RET_ROT_BASE), -jnp.linspace(0.0, 1.0, half, dtype=F32))
    ang = jnp.arange(seq).astype(F32)[:, None] * inv_freq[None, :]
    pos_tiles = seq // tm
    tab_spec = pl.BlockSpec((tm, half), lambda i, j: (i % pos_tiles, 0))
    return pl.pallas_call(
        functools.partial(_ret_proj_kernel, n_q_tiles=d // tn),
        grid=(t // tm, n // tn),
        in_specs=[
            pl.BlockSpec((tm, d), lambda i, j: (i, 0)),
            pl.BlockSpec((1, d), lambda i, j: (0, 0)),
            pl.BlockSpec((d, tn), lambda i, j: (0, j)),
            tab_spec, tab_spec,
        ],
        out_specs=pl.BlockSpec((tm, tn), lambda i, j: (i, j)),
        out_shape=jax.ShapeDtypeStruct((t, n), F32),
        scratch_shapes=[pltpu.VMEM((tm, d), BF16)],
        compiler_params=_params("parallel", "arbitrary"),
        name="ret_proj",
    )(x, gain.reshape(1, d), w_in, jnp.cos(ang), jnp.sin(ang))


def _ret_kernel(q_ref, k_ref, v_ref, g_ref, gn_ref, dmask_ref, qdec_ref, kdec_ref, cdec_ref, o_ref, state_ref):
    seq = q_ref.shape[0]
    c = RET_CHUNK
    state_ref[...] = jnp.zeros_like(state_ref)
    dmask = dmask_ref[0]
    qdec = qdec_ref[0]
    kdec = kdec_ref[0]
    cdec = cdec_ref[0]
    gn = gn_ref[...]

    def step(n, carry):
        rows = pl.ds(pl.multiple_of(n * c, c), c)
        qc = q_ref[rows, :].astype(BF16)
        kc = k_ref[rows, :]
        vc = v_ref[rows, :].astype(BF16)
        state = state_ref[...]
        s = _dot_nt(qc, kc.astype(BF16)) * dmask
        inner = _dot(s.astype(BF16), vc)
        cross = _dot(qc, state.astype(BF16)) * qdec
        kt = (kc * kdec).T.astype(BF16)
        state_ref[...] = state * cdec + _dot(kt, vc)
        out = inner + cross
        out = out * lax.rsqrt(jnp.mean(out * out, axis=-1, keepdims=True) + RMS_EPS) * gn
        g = g_ref[rows, :]
        o_ref[rows, :] = (g * jax.nn.sigmoid(g) * out).astype(o_ref.dtype)
        return carry

    lax.fori_loop(0, seq // c, step, 0)


def _retention(proj, gn_gain, batch, seq, d):
    heads = d // RET_HEAD_DIM
    c = RET_CHUNK
    log_gamma = jnp.log1p(-jnp.power(2.0, -5.0 - jnp.arange(heads, dtype=F32)))
    n = jnp.arange(c, dtype=F32)
    diff = n[:, None] - n[None, :]
    dmask = jnp.exp(jnp.where((diff >= 0)[None], diff[None] * log_gamma[:, None, None], -jnp.inf))
    qdec = jnp.exp((n[None, :] + 1.0) * log_gamma[:, None])[:, :, None]
    kdec = jnp.exp((c - 1.0 - n[None, :]) * log_gamma[:, None])[:, :, None]
    cdec = jnp.broadcast_to(jnp.exp(c * log_gamma)[:, None, None], (heads, 1, RET_HEAD_DIM))
    blk = (seq, RET_HEAD_DIM)
    return pl.pallas_call(
        _ret_kernel,
        grid=(batch, heads),
        in_specs=[
            pl.BlockSpec(blk, lambda b, h: (b, h)),
            pl.BlockSpec(blk, lambda b, h: (b, heads + h)),
            pl.BlockSpec(blk, lambda b, h: (b, 2 * heads + h)),
            pl.BlockSpec(blk, lambda b, h: (b, 3 * heads + h)),
            pl.BlockSpec((1, RET_HEAD_DIM), lambda b, h: (0, h)),
            pl.BlockSpec((1, c, c), lambda b, h: (h, 0, 0)),
            pl.BlockSpec((1, c, 1), lambda b, h: (h, 0, 0)),
            pl.BlockSpec((1, c, 1), lambda b, h: (h, 0, 0)),
            pl.BlockSpec((1, 1, RET_HEAD_DIM), lambda b, h: (h, 0, 0)),
        ],
        out_specs=pl.BlockSpec(blk, lambda b, h: (b, h)),
        out_shape=jax.ShapeDtypeStruct((batch * seq, d), BF16),
        scratch_shapes=[pltpu.VMEM((RET_HEAD_DIM, RET_HEAD_DIM), F32)],
        compiler_params=_params("parallel", "parallel"),
        name="retention",
    )(proj, proj, proj, proj, gn_gain.reshape(1, d), dmask, qdec, kdec, cdec)


def kernel(x, norm_gain, ffn_w_gate_up, ffn_w_down, moba_w_qkv, moba_w_o, ret_w_in, ret_w_o, ret_gn_gain, final_norm):
    batch, seq, d = x.shape
    depth = norm_gain.shape[0]
    h = x.reshape(batch * seq, d)

    def bf16(w):
        return w.astype(BF16)

    for i in range(depth):
        g = norm_gain[i]
        h = _ffn(h, g[0], bf16(ffn_w_gate_up[i, 0]), bf16(ffn_w_down[i, 0]), final_norm, final_norm=False)
        if i % 2 == 0:
            qkv = _moba_proj(h, g[1], bf16(moba_w_qkv[i // 2]), seq)
            h = _out_proj(_moba_attn(qkv, batch, seq, d), bf16(moba_w_o[i // 2]), h)
        else:
            proj = _ret_proj(h, g[1], bf16(ret_w_in[i // 2]), seq)
            h = _out_proj(_retention(proj, ret_gn_gain[i // 2], batch, seq, d), bf16(ret_w_o[i // 2]), h)
        h = _ffn(h, g[2], bf16(ffn_w_gate_up[i, 1]), bf16(ffn_w_down[i, 1]), final_norm,
                 final_norm=(i == depth - 1))
    return h.reshape(batch, seq, d)
```

```python
import functools

import jax
import jax.numpy as jnp
from jax import lax
from jax.experimental import pallas as pl
from jax.experimental.pallas import tpu as pltpu

FFN_RES = 0.5
RMS_EPS = 1e-6

MOBA_HEAD_DIM = 128
MOBA_BLOCK = 256
MOBA_TOPK = 3
ROPE_THETA = 500000.0
ROPE_DIM = MOBA_HEAD_DIM // 4

RET_HEAD_DIM = 256
RET_CHUNK = 256
RET_ROT_BASE = 10000.0

V7X_VMEM_BYTES = 64 * 1024 * 1024
VMEM_LIMIT_BYTES = V7X_VMEM_BYTES - 8 * 1024 * 1024

F32 = jnp.float32
BF16 = jnp.bfloat16


def _params(*semantics):
    return pltpu.CompilerParams(dimension_semantics=semantics, vmem_limit_bytes=VMEM_LIMIT_BYTES)


def _rms_norm(x, gain):
    return x * lax.rsqrt(jnp.mean(x * x, axis=-1, keepdims=True) + RMS_EPS) * gain


def _dot(a, b):
    return jnp.dot(a, b, preferred_element_type=F32)


def _dot_nt(a, b):
    return lax.dot_general(a, b, (((1,), (1,)), ((), ())), preferred_element_type=F32)


def _silu(g):
    return g * jax.nn.sigmoid(g)


def _ffn_kernel(x_ref, gain_ref, wg_ref, wu_ref, wd_ref, egain_ref, o_ref, *rest, epilogue):
    if epilogue == "norm_out":
        xn_out_ref, xn_ref = rest
    else:
        (xn_ref,) = rest
    j = pl.program_id(1)
    last_j = pl.num_programs(1) - 1

    def step(first, last):
        if first:
            xn = _rms_norm(x_ref[...], gain_ref[...]).astype(BF16)
            xn_ref[...] = xn
        else:
            xn = xn_ref[...]
        g = _dot(xn, wg_ref[...].astype(BF16))
        u = _dot(xn, wu_ref[...].astype(BF16))
        acc = _dot((_silu(g) * u).astype(BF16), wd_ref[...].astype(BF16))
        if not first:
            acc = o_ref[...] + acc
        if last:
            y = x_ref[...] + FFN_RES * acc
            if epilogue == "final_norm":
                y = _rms_norm(y, egain_ref[...])
            elif epilogue == "norm_out":
                xn_out_ref[...] = _rms_norm(y, egain_ref[...]).astype(BF16)
            o_ref[...] = y
        else:
            o_ref[...] = acc

    pl.when(j == 0)(lambda: step(True, False))
    pl.when((j > 0) & (j < last_j))(lambda: step(False, False))
    pl.when(j == last_j)(lambda: step(False, True))


def _ffn(x, gain, w_gate_up, w_down, layer, which, egain, *, epilogue, tm=1024, tf=256):
    t, d = x.shape
    f = w_down.shape[2]
    nf = f // tf
    assert nf >= 3
    row = pl.BlockSpec((tm, d), lambda i, j: (i, 0))
    out_shape = [jax.ShapeDtypeStruct((t, d), F32)]
    out_specs = [row]
    if epilogue == "norm_out":
        out_shape.append(jax.ShapeDtypeStruct((t, d), BF16))
        out_specs.append(row)
    return pl.pallas_call(
        functools.partial(_ffn_kernel, epilogue=epilogue),
        grid=(t // tm, nf),
        in_specs=[
            pl.BlockSpec((tm, d), lambda i, j: (i, 0), pipeline_mode=pl.Buffered(1)),
            pl.BlockSpec((1, d), lambda i, j: (0, 0)),
            pl.BlockSpec((None, None, d, tf), lambda i, j: (layer, which, 0, j)),
            pl.BlockSpec((None, None, d, tf), lambda i, j: (layer, which, 0, j + nf)),
            pl.BlockSpec((None, None, tf, d), lambda i, j: (layer, which, j, 0)),
            pl.BlockSpec((1, d), lambda i, j: (0, 0)),
        ],
        out_specs=out_specs,
        out_shape=out_shape,
        scratch_shapes=[pltpu.VMEM((tm, d), BF16)],
        compiler_params=_params("parallel", "arbitrary"),
        name="ffn",
    )(x, gain.reshape(1, d), w_gate_up, w_gate_up, w_down, egain.reshape(1, d))


def _out_proj_kernel(a_ref, w_ref, x_ref, o_ref, wb_ref):
    @pl.when(pl.program_id(0) == 0)
    def _():
        wb_ref[...] = w_ref[...].astype(BF16)

    o_ref[...] = x_ref[...] + _dot(a_ref[...], wb_ref[...])


def _out_proj(a, w, layer, x, *, tm=512):
    t, k = a.shape
    n = w.shape[2]
    return pl.pallas_call(
        _out_proj_kernel,
        grid=(t // tm,),
        in_specs=[
            pl.BlockSpec((tm, k), lambda i: (i, 0)),
            pl.BlockSpec((None, k, n), lambda i: (layer, 0, 0), pipeline_mode=pl.Buffered(1)),
            pl.BlockSpec((tm, n), lambda i: (i, 0)),
        ],
        out_specs=pl.BlockSpec((tm, n), lambda i: (i, 0)),
        out_shape=jax.ShapeDtypeStruct((t, n), F32),
        scratch_shapes=[pltpu.VMEM((k, n), BF16)],
        compiler_params=_params("arbitrary"),
        name="out_proj",
    )(a, w, x)


def _moba_rope_tables(seq):
    half = ROPE_DIM // 2
    inv_freq = jnp.power(jnp.float32(ROPE_THETA), -jnp.arange(half, dtype=F32) / half)
    ang = jnp.arange(seq).astype(F32)[:, None] * inv_freq[None, :]
    cos, sin = jnp.cos(ang), jnp.sin(ang)
    rest = MOBA_HEAD_DIM - ROPE_DIM
    c = jnp.concatenate([cos, cos, jnp.ones((seq, rest), F32)], axis=-1)
    a = jnp.concatenate([-sin, jnp.zeros((seq, half + rest), F32)], axis=-1)
    b = jnp.concatenate([jnp.zeros((seq, half), F32), sin, jnp.zeros((seq, rest), F32)], axis=-1)
    scale = MOBA_HEAD_DIM ** -0.5
    return tuple(jnp.stack([t * scale, t]) for t in (c, a, b))


def _moba_proj_kernel(xn_ref, w_ref, c_ref, a_ref, b_ref, o_ref, wb_ref, *, n_rot_tiles):
    j = pl.program_id(0)

    @pl.when(pl.program_id(1) == 0)
    def _():
        wb_ref[...] = w_ref[...].astype(BF16)

    @pl.when(j < n_rot_tiles)
    def _():
        y = _dot(xn_ref[...], wb_ref[...])
        half = ROPE_DIM // 2
        c, a, b = c_ref[...], a_ref[...], b_ref[...]
        for h in range(y.shape[1] // MOBA_HEAD_DIM):
            sl = slice(h * MOBA_HEAD_DIM, (h + 1) * MOBA_HEAD_DIM)
            yh = y[:, sl]
            up = pltpu.roll(yh, MOBA_HEAD_DIM - half, 1)
            down = pltpu.roll(yh, half, 1)
            o_ref[:, sl] = (yh * c + up * a + down * b).astype(o_ref.dtype)

    @pl.when(j >= n_rot_tiles)
    def _():
        o_ref[...] = _dot(xn_ref[...], wb_ref[...]).astype(o_ref.dtype)


def _moba_proj(xn, w_qkv, layer, seq, *, tm=1024, tn=1024):
    t, d = xn.shape
    n = w_qkv.shape[2]
    tn = min(tn, d)
    c, a, b = _moba_rope_tables(seq)
    pos_tiles = seq // tm
    q_tiles = d // tn
    tab_spec = pl.BlockSpec((None, tm, MOBA_HEAD_DIM),
                            lambda j, i: (jnp.minimum(j // q_tiles, 1), i % pos_tiles, 0))
    return pl.pallas_call(
        functools.partial(_moba_proj_kernel, n_rot_tiles=2 * q_tiles),
        grid=(n // tn, t // tm),
        in_specs=[
            pl.BlockSpec((tm, d), lambda j, i: (i, 0)),
            pl.BlockSpec((None, d, tn), lambda j, i: (layer, 0, j)),
            tab_spec, tab_spec, tab_spec,
        ],
        out_specs=pl.BlockSpec((tm, tn), lambda j, i: (i, j)),
        out_shape=jax.ShapeDtypeStruct((t, n), BF16),
        scratch_shapes=[pltpu.VMEM((d, tn), BF16)],
        compiler_params=_params("arbitrary", "arbitrary"),
        name="moba_proj",
    )(xn, w_qkv, c, a, b)


def _split_bf16(x):
    hi = x.astype(BF16)
    lo = (x - hi.astype(F32)).astype(BF16)
    return hi, lo


def _moba_attn_kernel(q_ref, k_ref, v_ref, o_ref):
    seq = q_ref.shape[0]
    blk = MOBA_BLOCK
    n_blk = seq // blk

    k16 = k_ref[...]
    vt16 = v_ref[...].astype(F32).T.astype(BF16)
    k_mean = jnp.mean(k16.astype(F32).reshape(n_blk, blk, MOBA_HEAD_DIM), axis=1)
    km_hi, km_lo = _split_bf16(k_mean)
    km16 = jnp.concatenate([km_hi, km_lo], axis=0)

    blk_row = lax.broadcasted_iota(jnp.int32, (n_blk, blk), 0)
    key_in_blk = lax.broadcasted_iota(jnp.int32, (blk, blk), 0)
    qry_in_blk = lax.broadcasted_iota(jnp.int32, (blk, blk), 1)
    causal = key_in_blk <= qry_in_blk

    for i in range(n_blk):
        q16 = q_ref[i * blk:(i + 1) * blk, :]
        n_keys = (i + 1) * blk
        st = _dot_nt(k16[:n_keys], q16)

        if i > MOBA_TOPK:
            g2 = _dot_nt(km16, q16)
            gate = g2[:n_blk] + g2[n_blk:]
            past = blk_row < i

        masked = []
        for j in range(i + 1):
            s = st[j * blk:(j + 1) * blk]
            if j == i:
                s = jnp.where(causal, s, -jnp.inf)
            elif i > MOBA_TOPK:
                gj = gate[j:j + 1, :]
                beats = ((gate > gj) | ((gate == gj) & (blk_row < j))) & past
                rank = jnp.sum(beats.astype(F32), axis=0, keepdims=True)
                s = jnp.where(rank < MOBA_TOPK, s, -jnp.inf)
            masked.append(s)

        m = masked[0]
        for s in masked[1:]:
            m = jnp.maximum(m, s)
        m = jnp.max(m, axis=0, keepdims=True)
        p = [jnp.exp(s - m) for s in masked]
        l = p[0]
        for pj in p[1:]:
            l = l + pj
        l = jnp.sum(l, axis=0, keepdims=True)
        pt = jnp.concatenate([pj.astype(BF16) for pj in p], axis=0)
        ot = _dot(vt16[:, :n_keys], pt) / l
        o_ref[i * blk:(i + 1) * blk, :] = ot.T.astype(o_ref.dtype)


def _moba_attn(qkv, batch, seq, d):
    heads = d // MOBA_HEAD_DIM
    blk = (seq, MOBA_HEAD_DIM)
    return pl.pallas_call(
        _moba_attn_kernel,
        grid=(batch, heads),
        in_specs=[
            pl.BlockSpec(blk, lambda b, h: (b, h)),
            pl.BlockSpec(blk, lambda b, h: (b, heads + h)),
            pl.BlockSpec(blk, lambda b, h: (b, 2 * heads + h)),
        ],
        out_specs=pl.BlockSpec(blk, lambda b, h: (b, h)),
        out_shape=jax.ShapeDtypeStruct((batch * seq, d), BF16),
        compiler_params=_params("parallel", "parallel"),
        name="moba_attn",
    )(qkv, qkv, qkv)


def _ret_proj_kernel(xn_ref, w_ref, cos_ref, sin_ref, o_ref, wb_ref, *, n_q_tiles):
    j = pl.program_id(0)
    half = RET_HEAD_DIM // 2

    @pl.when(pl.program_id(1) == 0)
    def _():
        wb_ref[...] = w_ref[...].astype(BF16)

    def rotate(scale):
        y = _dot(xn_ref[...], wb_ref[...])
        cos, sin = cos_ref[...], sin_ref[...]
        for h in range(y.shape[1] // RET_HEAD_DIM):
            lo = h * RET_HEAD_DIM
            x1, x2 = y[:, lo:lo + half], y[:, lo + half:lo + RET_HEAD_DIM]
            o_ref[:, lo:lo + half] = ((x1 * cos - x2 * sin) * scale).astype(o_ref.dtype)
            o_ref[:, lo + half:lo + RET_HEAD_DIM] = ((x1 * sin + x2 * cos) * scale).astype(o_ref.dtype)

    pl.when(j < n_q_tiles)(lambda: rotate(1.0))
    pl.when((j >= n_q_tiles) & (j < 2 * n_q_tiles))(lambda: rotate(RET_HEAD_DIM ** -0.5))

    @pl.when(j >= 2 * n_q_tiles)
    def _():
        o_ref[...] = _dot(xn_ref[...], wb_ref[...]).astype(o_ref.dtype)


def _ret_proj(xn, w_in, layer, seq, *, tm=1024, tn=1024):
    t, d = xn.shape
    n = w_in.shape[2]
    tn = min(tn, d)
    half = RET_HEAD_DIM // 2
    inv_freq = jnp.power(jnp.float32(RET_ROT_BASE), -jnp.linspace(0.0, 1.0, half, dtype=F32))
    ang = jnp.arange(seq).astype(F32)[:, None] * inv_freq[None, :]
    pos_tiles = seq // tm
    tab_spec = pl.BlockSpec((tm, half), lambda j, i: (i % pos_tiles, 0))
    return pl.pallas_call(
        functools.partial(_ret_proj_kernel, n_q_tiles=d // tn),
        grid=(n // tn, t // tm),
        in_specs=[
            pl.BlockSpec((tm, d), lambda j, i: (i, 0)),
            pl.BlockSpec((None, d, tn), lambda j, i: (layer, 0, j)),
            tab_spec, tab_spec,
        ],
        out_specs=pl.BlockSpec((tm, tn), lambda j, i: (i, j)),
        out_shape=jax.ShapeDtypeStruct((t, n), BF16),
        scratch_shapes=[pltpu.VMEM((d, tn), BF16)],
        compiler_params=_params("arbitrary", "arbitrary"),
        name="ret_proj",
    )(xn, w_in, jnp.cos(ang), jnp.sin(ang))


def _ret_kernel(q_ref, k_ref, v_ref, g_ref, gn_ref, dmask_ref, qdec_ref, kdec_ref, cdec_ref, o_ref):
    seq = q_ref.shape[0]
    c = RET_CHUNK
    n_chunks = seq // c
    dmask = dmask_ref[...]
    qdec = qdec_ref[...]
    kdec = kdec_ref[...]
    cdec = cdec_ref[...]
    gn = gn_ref[...]

    state = None
    for n in range(n_chunks):
        rows = slice(n * c, (n + 1) * c)
        qc, kc, vc = q_ref[rows, :], k_ref[rows, :], v_ref[rows, :]
        s = _dot_nt(qc, kc) * dmask
        out = _dot(s.astype(BF16), vc)
        if state is not None:
            out = out + _dot(qc, state.astype(BF16)) * qdec
        if n + 1 < n_chunks:
            kt = (kc.astype(F32) * kdec).T.astype(BF16)
            upd = _dot(kt, vc)
            state = upd if state is None else state * cdec + upd
        out = out * lax.rsqrt(jnp.mean(out * out, axis=-1, keepdims=True) + RMS_EPS) * gn
        o_ref[rows, :] = (_silu(g_ref[rows, :].astype(F32)) * out).astype(o_ref.dtype)


def _retention(proj, gn_gain, layer, batch, seq, d):
    heads = d // RET_HEAD_DIM
    c = RET_CHUNK
    log_gamma = jnp.log1p(-jnp.power(2.0, -5.0 - jnp.arange(heads, dtype=F32)))
    n = jnp.arange(c, dtype=F32)
    diff = n[:, None] - n[None, :]
    dmask = jnp.exp(jnp.where((diff >= 0)[None], diff[None] * log_gamma[:, None, None], -jnp.inf))
    qdec = jnp.exp((n[None, :] + 1.0) * log_gamma[:, None])[:, :, None]
    kdec = jnp.exp((c - 1.0 - n[None, :]) * log_gamma[:, None])[:, :, None]
    cdec = jnp.broadcast_to(jnp.exp(c * log_gamma)[:, None, None], (heads, 1, RET_HEAD_DIM))
    blk = (seq, RET_HEAD_DIM)
    return pl.pallas_call(
        _ret_kernel,
        grid=(batch, heads),
        in_specs=[
            pl.BlockSpec(blk, lambda b, h: (b, h)),
            pl.BlockSpec(blk, lambda b, h: (b, heads + h)),
            pl.BlockSpec(blk, lambda b, h: (b, 2 * heads + h)),
            pl.BlockSpec(blk, lambda b, h: (b, 3 * heads + h)),
            pl.BlockSpec((None, 1, RET_HEAD_DIM), lambda b, h: (layer, 0, h)),
            pl.BlockSpec((None, c, c), lambda b, h: (h, 0, 0)),
            pl.BlockSpec((None, c, 1), lambda b, h: (h, 0, 0)),
            pl.BlockSpec((None, c, 1), lambda b, h: (h, 0, 0)),
            pl.BlockSpec((None, 1, RET_HEAD_DIM), lambda b, h: (h, 0, 0)),
        ],
        out_specs=pl.BlockSpec(blk, lambda b, h: (b, h)),
        out_shape=jax.ShapeDtypeStruct((batch * seq, d), BF16),
        compiler_params=_params("parallel", "parallel"),
        name="retention",
    )(proj, proj, proj, proj, gn_gain.reshape(gn_gain.shape[0], 1, d), dmask, qdec, kdec, cdec)


def kernel(x, norm_gain, ffn_w_gate_up, ffn_w_down, moba_w_qkv, moba_w_o, ret_w_in, ret_w_o, ret_gn_gain, final_norm):
    batch, seq, d = x.shape
    depth = norm_gain.shape[0]
    h = x.reshape(batch * seq, d)
    for i in range(depth):
        g = norm_gain[i]
        h, hn = _ffn(h, g[0], ffn_w_gate_up, ffn_w_down, i, 0, g[1], epilogue="norm_out")
        if i % 2 == 0:
            qkv = _moba_proj(hn, moba_w_qkv, i // 2, seq)
            h = _out_proj(_moba_attn(qkv, batch, seq, d), moba_w_o, i // 2, h)
        else:
            proj = _ret_proj(hn, ret_w_in, i // 2, seq)
            h = _out_proj(_retention(proj, ret_gn_gain, i // 2, batch, seq, d), ret_w_o, i // 2, h)
        h = _ffn(h, g[2], ffn_w_gate_up, ffn_w_down, i, 1, final_norm,
                 epilogue="final_norm" if i == depth - 1 else "none")[0]
    return h.reshape(batch, seq, d)
```

```python
import functools

import jax
import jax.numpy as jnp
from jax import lax
from jax.experimental import pallas as pl
from jax.experimental.pallas import tpu as pltpu

FFN_RES = 0.5
RMS_EPS = 1e-6

MOBA_HEAD_DIM = 128
MOBA_BLOCK = 256
MOBA_TOPK = 3
ROPE_THETA = 500000.0
ROPE_DIM = MOBA_HEAD_DIM // 4

RET_HEAD_DIM = 256
RET_CHUNK = 256
RET_ROT_BASE = 10000.0

V7X_VMEM_BYTES = 64 * 1024 * 1024
VMEM_LIMIT_BYTES = V7X_VMEM_BYTES - 8 * 1024 * 1024

F32 = jnp.float32
BF16 = jnp.bfloat16


def _params(*semantics):
    return pltpu.CompilerParams(dimension_semantics=semantics, vmem_limit_bytes=VMEM_LIMIT_BYTES)


def _rms_norm(x, gain):
    return x * lax.rsqrt(jnp.mean(x * x, axis=-1, keepdims=True) + RMS_EPS) * gain


def _dot(a, b):
    return jnp.dot(a, b, preferred_element_type=F32)


def _dot_nt(a, b):
    return lax.dot_general(a, b, (((1,), (1,)), ((), ())), preferred_element_type=F32)


def _silu(g):
    return g * jax.nn.sigmoid(g)


def _ffn_step(j, last_j, x_ref, gain_ref, egain_ref, o_ref, xn_out_ref, xn_ref, weights, epilogue):
    def step(first, last):
        if first:
            xn = _rms_norm(x_ref[...], gain_ref[...]).astype(BF16)
            xn_ref[...] = xn
        else:
            xn = xn_ref[...]
        wgu, wd = weights()
        tf = wd.shape[0]
        gu = _dot(xn, wgu)
        acc = _dot((_silu(gu[:, :tf]) * gu[:, tf:]).astype(BF16), wd)
        if not first:
            acc = o_ref[...] + acc
        if last:
            y = x_ref[...] + FFN_RES * acc
            if epilogue == "final_norm":
                y = _rms_norm(y, egain_ref[...])
            elif epilogue == "norm_out":
                xn_out_ref[...] = _rms_norm(y, egain_ref[...]).astype(BF16)
            o_ref[...] = y
        else:
            o_ref[...] = acc

    pl.when(j == 0)(lambda: step(True, False))
    pl.when((j > 0) & (j < last_j))(lambda: step(False, False))
    pl.when(j == last_j)(lambda: step(False, True))


def _ffn_head_kernel(x_ref, gain_ref, wg_ref, wu_ref, wd_ref, egain_ref, *rest, epilogue):
    if epilogue == "norm_out":
        _, o_ref, xn_out_ref, wgu16_ref, wd16_ref, xn_ref = rest
    else:
        o_ref, wgu16_ref, wd16_ref, xn_ref = rest
        xn_out_ref = None
    tf = wd_ref.shape[0]

    def weights():
        wgu16_ref[:, :tf] = wg_ref[...].astype(BF16)
        wgu16_ref[:, tf:] = wu_ref[...].astype(BF16)
        wd16_ref[...] = wd_ref[...].astype(BF16)
        return wgu16_ref[...], wd16_ref[...]

    _ffn_step(pl.program_id(0), pl.num_programs(0) - 1, x_ref, gain_ref, egain_ref, o_ref, xn_out_ref, xn_ref,
              weights, epilogue)


def _ffn_tail_kernel(x_ref, gain_ref, wgu16_ref, wd16_ref, egain_ref, *rest, epilogue):
    if epilogue == "norm_out":
        _, o_ref, xn_out_ref, xn_ref = rest
    else:
        o_ref, xn_ref = rest
        xn_out_ref = None
    _ffn_step(pl.program_id(1), pl.num_programs(1) - 1, x_ref, gain_ref, egain_ref, o_ref, xn_out_ref, xn_ref,
              lambda: (wgu16_ref[...], wd16_ref[...]), epilogue)


def _ffn(x, gain, w_gate_up, w_down, layer, which, egain, *, epilogue, tm=1024, tf=256):
    t, d = x.shape
    f = w_down.shape[2]
    nf = f // tf
    assert nf >= 3
    norm_out = epilogue == "norm_out"
    once = pl.Buffered(1)
    any_spec = pl.BlockSpec(memory_space=pl.ANY)
    act_shapes = [jax.ShapeDtypeStruct((t, d), F32)] + ([jax.ShapeDtypeStruct((t, d), BF16)] if norm_out else [])
    extra_in = [jnp.zeros((t, d), BF16)] if norm_out else []

    first_rows = pl.BlockSpec((tm, d), lambda j: (0, 0), pipeline_mode=once)
    head = pl.pallas_call(
        functools.partial(_ffn_head_kernel, epilogue=epilogue),
        grid=(nf,),
        in_specs=[
            first_rows,
            pl.BlockSpec((1, d), lambda j: (0, 0)),
            pl.BlockSpec((None, None, d, tf), lambda j: (layer, which, 0, j)),
            pl.BlockSpec((None, None, d, tf), lambda j: (layer, which, 0, j + nf)),
            pl.BlockSpec((None, None, tf, d), lambda j: (layer, which, j, 0)),
            pl.BlockSpec((1, d), lambda j: (0, 0)),
        ] + [any_spec] * norm_out,
        out_specs=[first_rows] * len(act_shapes) + [
            pl.BlockSpec((None, d, 2 * tf), lambda j: (j, 0, 0)),
            pl.BlockSpec((None, tf, d), lambda j: (j, 0, 0)),
        ],
        out_shape=act_shapes + [jax.ShapeDtypeStruct((nf, d, 2 * tf), BF16),
                                jax.ShapeDtypeStruct((nf, tf, d), BF16)],
        input_output_aliases={0: 0, 6: 1} if norm_out else {0: 0},
        scratch_shapes=[pltpu.VMEM((tm, d), BF16)],
        compiler_params=_params("arbitrary"),
        name="ffn_head",
    )(x, gain.reshape(1, d), w_gate_up, w_gate_up, w_down, egain.reshape(1, d), *extra_in)
    *acts, wgu16, wd16 = head

    def rest_rows(**kw):
        return pl.BlockSpec((tm, d), lambda i, j: (i + 1, 0), **kw)

    return pl.pallas_call(
        functools.partial(_ffn_tail_kernel, epilogue=epilogue),
        grid=(t // tm - 1, nf),
        in_specs=[
            rest_rows(pipeline_mode=once),
            pl.BlockSpec((1, d), lambda i, j: (0, 0)),
            pl.BlockSpec((None, d, 2 * tf), lambda i, j: (j, 0, 0)),
            pl.BlockSpec((None, tf, d), lambda i, j: (j, 0, 0)),
            pl.BlockSpec((1, d), lambda i, j: (0, 0)),
        ] + [any_spec] * norm_out,
        out_specs=[rest_rows()] * len(act_shapes),
        out_shape=act_shapes,
        input_output_aliases={0: 0, 5: 1} if norm_out else {0: 0},
        scratch_shapes=[pltpu.VMEM((tm, d), BF16)],
        compiler_params=_params("parallel", "arbitrary"),
        name="ffn_tail",
    )(acts[0], gain.reshape(1, d), wgu16, wd16, egain.reshape(1, d), *acts[1:])


def _out_proj_kernel(a_ref, w_ref, x_ref, o_ref, wb_ref):
    @pl.when(pl.program_id(0) == 0)
    def _():
        wb_ref[...] = w_ref[...].astype(BF16)

    o_ref[...] = x_ref[...] + _dot(a_ref[...], wb_ref[...])


def _out_proj(a, w, layer, x, *, tm=512):
    t, k = a.shape
    n = w.shape[2]
    return pl.pallas_call(
        _out_proj_kernel,
        grid=(t // tm,),
        in_specs=[
            pl.BlockSpec((tm, k), lambda i: (i, 0)),
            pl.BlockSpec((None, k, n), lambda i: (layer, 0, 0), pipeline_mode=pl.Buffered(1)),
            pl.BlockSpec((tm, n), lambda i: (i, 0)),
        ],
        out_specs=pl.BlockSpec((tm, n), lambda i: (i, 0)),
        out_shape=jax.ShapeDtypeStruct((t, n), F32),
        scratch_shapes=[pltpu.VMEM((k, n), BF16)],
        compiler_params=_params("arbitrary"),
        name="out_proj",
    )(a, w, x)


def _moba_rope_tables(seq):
    half = ROPE_DIM // 2
    inv_freq = jnp.power(jnp.float32(ROPE_THETA), -jnp.arange(half, dtype=F32) / half)
    ang = jnp.arange(seq).astype(F32)[:, None] * inv_freq[None, :]
    cos, sin = jnp.cos(ang), jnp.sin(ang)
    rest = MOBA_HEAD_DIM - ROPE_DIM
    c = jnp.concatenate([cos, cos, jnp.ones((seq, rest), F32)], axis=-1)
    a = jnp.concatenate([-sin, jnp.zeros((seq, half + rest), F32)], axis=-1)
    b = jnp.concatenate([jnp.zeros((seq, half), F32), sin, jnp.zeros((seq, rest), F32)], axis=-1)
    scale = MOBA_HEAD_DIM ** -0.5
    return tuple(jnp.stack([t * scale, t]) for t in (c, a, b))


def _moba_proj_kernel(xn_ref, w_ref, c_ref, a_ref, b_ref, o_ref, wb_ref, *, n_rot_tiles):
    j = pl.program_id(0)

    @pl.when(pl.program_id(1) == 0)
    def _():
        wb_ref[...] = w_ref[...].astype(BF16)

    @pl.when(j < n_rot_tiles)
    def _():
        y = _dot(xn_ref[...], wb_ref[...])
        half = ROPE_DIM // 2
        c, a, b = c_ref[...], a_ref[...], b_ref[...]
        for h in range(y.shape[1] // MOBA_HEAD_DIM):
            sl = slice(h * MOBA_HEAD_DIM, (h + 1) * MOBA_HEAD_DIM)
            yh = y[:, sl]
            up = pltpu.roll(yh, MOBA_HEAD_DIM - half, 1)
            down = pltpu.roll(yh, half, 1)
            o_ref[:, sl] = (yh * c + up * a + down * b).astype(o_ref.dtype)

    @pl.when(j >= n_rot_tiles)
    def _():
        o_ref[...] = _dot(xn_ref[...], wb_ref[...]).astype(o_ref.dtype)


def _moba_proj(xn, w_qkv, layer, seq, *, tm=1024, tn=1024):
    t, d = xn.shape
    n = w_qkv.shape[2]
    tn = min(tn, d)
    c, a, b = _moba_rope_tables(seq)
    pos_tiles = seq // tm
    q_tiles = d // tn
    tab_spec = pl.BlockSpec((None, tm, MOBA_HEAD_DIM),
                            lambda j, i: (jnp.minimum(j // q_tiles, 1), i % pos_tiles, 0))
    return pl.pallas_call(
        functools.partial(_moba_proj_kernel, n_rot_tiles=2 * q_tiles),
        grid=(n // tn, t // tm),
        in_specs=[
            pl.BlockSpec((tm, d), lambda j, i: (i, 0)),
            pl.BlockSpec((None, d, tn), lambda j, i: (layer, 0, j)),
            tab_spec, tab_spec, tab_spec,
        ],
        out_specs=pl.BlockSpec((tm, tn), lambda j, i: (i, j)),
        out_shape=jax.ShapeDtypeStruct((t, n), BF16),
        scratch_shapes=[pltpu.VMEM((d, tn), BF16)],
        compiler_params=_params("arbitrary", "arbitrary"),
        name="moba_proj",
    )(xn, w_qkv, c, a, b)


def _split_bf16(x):
    hi = x.astype(BF16)
    lo = (x - hi.astype(F32)).astype(BF16)
    return hi, lo


def _moba_attn_kernel(q_ref, k_ref, v_ref, o_ref):
    seq = q_ref.shape[0]
    blk = MOBA_BLOCK
    n_blk = seq // blk

    k16 = k_ref[...]
    vt16 = v_ref[...].astype(F32).T.astype(BF16)
    k_mean = jnp.mean(k16.astype(F32).reshape(n_blk, blk, MOBA_HEAD_DIM), axis=1)
    km_hi, km_lo = _split_bf16(k_mean)
    km16 = jnp.concatenate([km_hi, km_lo], axis=0)

    blk_row = lax.broadcasted_iota(jnp.int32, (n_blk, blk), 0)
    key_in_blk = lax.broadcasted_iota(jnp.int32, (blk, blk), 0)
    qry_in_blk = lax.broadcasted_iota(jnp.int32, (blk, blk), 1)
    causal = key_in_blk <= qry_in_blk

    for i in range(n_blk):
        q16 = q_ref[i * blk:(i + 1) * blk, :]
        n_keys = (i + 1) * blk
        st = _dot_nt(k16[:n_keys], q16)

        if i > MOBA_TOPK:
            g2 = _dot_nt(km16, q16)
            gate = g2[:n_blk] + g2[n_blk:]
            past = blk_row < i

        masked = []
        for j in range(i + 1):
            s = st[j * blk:(j + 1) * blk]
            if j == i:
                s = jnp.where(causal, s, -jnp.inf)
            elif i > MOBA_TOPK:
                gj = gate[j:j + 1, :]
                beats = ((gate > gj) | ((gate == gj) & (blk_row < j))) & past
                rank = jnp.sum(beats.astype(F32), axis=0, keepdims=True)
                s = jnp.where(rank < MOBA_TOPK, s, -jnp.inf)
            masked.append(s)

        m = masked[0]
        for s in masked[1:]:
            m = jnp.maximum(m, s)
        m = jnp.max(m, axis=0, keepdims=True)
        p = [jnp.exp(s - m) for s in masked]
        l = p[0]
        for pj in p[1:]:
            l = l + pj
        l = jnp.sum(l, axis=0, keepdims=True)
        pt = jnp.concatenate([pj.astype(BF16) for pj in p], axis=0)
        ot = _dot(vt16[:, :n_keys], pt) / l
        o_ref[i * blk:(i + 1) * blk, :] = ot.T.astype(o_ref.dtype)


def _moba_attn(qkv, batch, seq, d):
    heads = d // MOBA_HEAD_DIM
    blk = (seq, MOBA_HEAD_DIM)
    return pl.pallas_call(
        _moba_attn_kernel,
        grid=(batch, heads),
        in_specs=[
            pl.BlockSpec(blk, lambda b, h: (b, h)),
            pl.BlockSpec(blk, lambda b, h: (b, heads + h)),
            pl.BlockSpec(blk, lambda b, h: (b, 2 * heads + h)),
        ],
        out_specs=pl.BlockSpec(blk, lambda b, h: (b, h)),
        out_shape=jax.ShapeDtypeStruct((batch * seq, d), BF16),
        compiler_params=_params("parallel", "parallel"),
        name="moba_attn",
    )(qkv, qkv, qkv)


def _ret_proj_kernel(xn_ref, w_ref, cos_ref, sin_ref, o_ref, wb_ref, *, n_q_tiles):
    j = pl.program_id(0)
    half = RET_HEAD_DIM // 2

    @pl.when(pl.program_id(1) == 0)
    def _():
        wb_ref[...] = w_ref[...].astype(BF16)

    def rotate(scale):
        y = _dot(xn_ref[...], wb_ref[...])
        cos, sin = cos_ref[...], sin_ref[...]
        for h in range(y.shape[1] // RET_HEAD_DIM):
            lo = h * RET_HEAD_DIM
            x1, x2 = y[:, lo:lo + half], y[:, lo + half:lo + RET_HEAD_DIM]
            o_ref[:, lo:lo + half] = ((x1 * cos - x2 * sin) * scale).astype(o_ref.dtype)
            o_ref[:, lo + half:lo + RET_HEAD_DIM] = ((x1 * sin + x2 * cos) * scale).astype(o_ref.dtype)

    pl.when(j < n_q_tiles)(lambda: rotate(1.0))
    pl.when((j >= n_q_tiles) & (j < 2 * n_q_tiles))(lambda: rotate(RET_HEAD_DIM ** -0.5))

    @pl.when(j >= 2 * n_q_tiles)
    def _():
        o_ref[...] = _dot(xn_ref[...], wb_ref[...]).astype(o_ref.dtype)


def _ret_proj(xn, w_in, layer, seq, *, tm=1024, tn=1024):
    t, d = xn.shape
    n = w_in.shape[2]
    tn = min(tn, d)
    half = RET_HEAD_DIM // 2
    inv_freq = jnp.power(jnp.float32(RET_ROT_BASE), -jnp.linspace(0.0, 1.0, half, dtype=F32))
    ang = jnp.arange(seq).astype(F32)[:, None] * inv_freq[None, :]
    pos_tiles = seq // tm
    tab_spec = pl.BlockSpec((tm, half), lambda j, i: (i % pos_tiles, 0))
    return pl.pallas_call(
        functools.partial(_ret_proj_kernel, n_q_tiles=d // tn),
        grid=(n // tn, t // tm),
        in_specs=[
            pl.BlockSpec((tm, d), lambda j, i: (i, 0)),
            pl.BlockSpec((None, d, tn), lambda j, i: (layer, 0, j)),
            tab_spec, tab_spec,
        ],
        out_specs=pl.BlockSpec((tm, tn), lambda j, i: (i, j)),
        out_shape=jax.ShapeDtypeStruct((t, n), BF16),
        scratch_shapes=[pltpu.VMEM((d, tn), BF16)],
        compiler_params=_params("arbitrary", "arbitrary"),
        name="ret_proj",
    )(xn, w_in, jnp.cos(ang), jnp.sin(ang))


def _ret_kernel(q_ref, k_ref, v_ref, g_ref, gn_ref, dmask_ref, qdec_ref, kdec_ref, cdec_ref, o_ref):
    seq = q_ref.shape[0]
    c = RET_CHUNK
    n_chunks = seq // c
    dmask = dmask_ref[...]
    qdec = qdec_ref[...]
    kdec = kdec_ref[...]
    cdec = cdec_ref[...]
    gn = gn_ref[...]

    state = None
    for n in range(n_chunks):
        rows = slice(n * c, (n + 1) * c)
        qc, kc, vc = q_ref[rows, :], k_ref[rows, :], v_ref[rows, :]
        s = _dot_nt(qc, kc) * dmask
        out = _dot(s.astype(BF16), vc)
        if state is not None:
            out = out + _dot(qc, state.astype(BF16)) * qdec
        if n + 1 < n_chunks:
            kt = (kc.astype(F32) * kdec).T.astype(BF16)
            upd = _dot(kt, vc)
            state = upd if state is None else state * cdec + upd
        out = out * lax.rsqrt(jnp.mean(out * out, axis=-1, keepdims=True) + RMS_EPS) * gn
        o_ref[rows, :] = (_silu(g_ref[rows, :].astype(F32)) * out).astype(o_ref.dtype)


def _retention(proj, gn_gain, layer, batch, seq, d):
    heads = d // RET_HEAD_DIM
    c = RET_CHUNK
    log_gamma = jnp.log1p(-jnp.power(2.0, -5.0 - jnp.arange(heads, dtype=F32)))
    n = jnp.arange(c, dtype=F32)
    diff = n[:, None] - n[None, :]
    dmask = jnp.exp(jnp.where((diff >= 0)[None], diff[None] * log_gamma[:, None, None], -jnp.inf))
    qdec = jnp.exp((n[None, :] + 1.0) * log_gamma[:, None])[:, :, None]
    kdec = jnp.exp((c - 1.0 - n[None, :]) * log_gamma[:, None])[:, :, None]
    cdec = jnp.broadcast_to(jnp.exp(c * log_gamma)[:, None, None], (heads, 1, RET_HEAD_DIM))
    blk = (seq, RET_HEAD_DIM)
    return pl.pallas_call(
        _ret_kernel,
        grid=(batch, heads),
        in_specs=[
            pl.BlockSpec(blk, lambda b, h: (b, h)),
            pl.BlockSpec(blk, lambda b, h: (b, heads + h)),
            pl.BlockSpec(blk, lambda b, h: (b, 2 * heads + h)),
            pl.BlockSpec(blk, lambda b, h: (b, 3 * heads + h)),
            pl.BlockSpec((None, 1, RET_HEAD_DIM), lambda b, h: (layer, 0, h)),
            pl.BlockSpec((None, c, c), lambda b, h: (h, 0, 0)),
            pl.BlockSpec((None, c, 1), lambda b, h: (h, 0, 0)),
            pl.BlockSpec((None, c, 1), lambda b, h: (h, 0, 0)),
            pl.BlockSpec((None, 1, RET_HEAD_DIM), lambda b, h: (h, 0, 0)),
        ],
        out_specs=pl.BlockSpec(blk, lambda b, h: (b, h)),
        out_shape=jax.ShapeDtypeStruct((batch * seq, d), BF16),
        compiler_params=_params("parallel", "parallel"),
        name="retention",
    )(proj, proj, proj, proj, gn_gain.reshape(gn_gain.shape[0], 1, d), dmask, qdec, kdec, cdec)


def kernel(x, norm_gain, ffn_w_gate_up, ffn_w_down, moba_w_qkv, moba_w_o, ret_w_in, ret_w_o, ret_gn_gain, final_norm):
    batch, seq, d = x.shape
    depth = norm_gain.shape[0]
    h = x.reshape(batch * seq, d)
    for i in range(depth):
        g = norm_gain[i]
        h, hn = _ffn(h, g[0], ffn_w_gate_up, ffn_w_down, i, 0, g[1], epilogue="norm_out")
        if i % 2 == 0:
            qkv = _moba_proj(hn, moba_w_qkv, i // 2, seq)
            h = _out_proj(_moba_attn(qkv, batch, seq, d), moba_w_o, i // 2, h)
        else:
            proj = _ret_proj(hn, ret_w_in, i // 2, seq)
            h = _out_proj(_retention(proj, ret_gn_gain, i // 2, batch, seq, d), ret_w_o, i // 2, h)
        h = _ffn(h, g[2], ffn_w_gate_up, ffn_w_down, i, 1, final_norm,
                 epilogue="final_norm" if i == depth - 1 else "none")[0]
    return h.reshape(batch, seq, d)
```

```python
import functools

import jax
import jax.numpy as jnp
from jax import lax
from jax.experimental import pallas as pl
from jax.experimental.pallas import tpu as pltpu

FFN_RES = 0.5
RMS_EPS = 1e-6

MOBA_HEAD_DIM = 128
MOBA_BLOCK = 256
MOBA_TOPK = 3
ROPE_THETA = 500000.0
ROPE_DIM = MOBA_HEAD_DIM // 4

RET_HEAD_DIM = 256
RET_CHUNK = 256
RET_ROT_BASE = 10000.0

V7X_VMEM_BYTES = 64 * 1024 * 1024
VMEM_LIMIT_BYTES = V7X_VMEM_BYTES - 8 * 1024 * 1024

F32 = jnp.float32
BF16 = jnp.bfloat16


def _params(*semantics):
    return pltpu.CompilerParams(dimension_semantics=semantics, vmem_limit_bytes=VMEM_LIMIT_BYTES)


def _rms_norm(x, gain):
    return x * lax.rsqrt(jnp.mean(x * x, axis=-1, keepdims=True) + RMS_EPS) * gain


def _dot(a, b):
    return jnp.dot(a, b, preferred_element_type=F32)


def _dot_nt(a, b):
    return lax.dot_general(a, b, (((1,), (1,)), ((), ())), preferred_element_type=F32)


def _silu(g):
    return g * jax.nn.sigmoid(g)


def _ffn_step(j, last_j, x_ref, gain_ref, egain_ref, o_ref, xn_out_ref, xn_ref, weights, epilogue):
    def step(first, last):
        if first:
            xn = _rms_norm(x_ref[...], gain_ref[...]).astype(BF16)
            xn_ref[...] = xn
        else:
            xn = xn_ref[...]
        tiles = weights()
        hidden = []
        for wgu, wd in tiles:
            tf = wd.shape[0]
            gu = _dot(xn, wgu)
            hidden.append((_silu(gu[:, :tf]) * gu[:, tf:]).astype(BF16))
        acc = _dot(jnp.concatenate(hidden, axis=1), jnp.concatenate([wd for _, wd in tiles], axis=0))
        if not first:
            acc = o_ref[...] + acc
        if last:
            y = x_ref[...] + FFN_RES * acc
            if epilogue == "final_norm":
                y = _rms_norm(y, egain_ref[...])
            elif epilogue == "norm_out":
                xn_out_ref[...] = _rms_norm(y, egain_ref[...]).astype(BF16)
            o_ref[...] = y
        else:
            o_ref[...] = acc

    pl.when(j == 0)(lambda: step(True, False))
    pl.when((j > 0) & (j < last_j))(lambda: step(False, False))
    pl.when(j == last_j)(lambda: step(False, True))


def _ffn_head_kernel(x_ref, gain_ref, wg_ref, wu_ref, wd_ref, egain_ref, *rest, epilogue):
    if epilogue == "norm_out":
        _, o_ref, xn_out_ref, wgu16_ref, wd16_ref, xn_ref = rest
    else:
        o_ref, wgu16_ref, wd16_ref, xn_ref = rest
        xn_out_ref = None
    tf = wd_ref.shape[0]

    def weights():
        wgu16_ref[:, :tf] = wg_ref[...].astype(BF16)
        wgu16_ref[:, tf:] = wu_ref[...].astype(BF16)
        wd16_ref[...] = wd_ref[...].astype(BF16)
        return [(wgu16_ref[...], wd16_ref[...])]

    _ffn_step(pl.program_id(0), pl.num_programs(0) - 1, x_ref, gain_ref, egain_ref, o_ref, xn_out_ref, xn_ref,
              weights, epilogue)


def _ffn_tail_kernel(x_ref, gain_ref, wgu16_ref, wd16_ref, egain_ref, *rest, epilogue):
    if epilogue == "norm_out":
        _, o_ref, xn_out_ref, xn_ref = rest
    else:
        o_ref, xn_ref = rest
        xn_out_ref = None
    _ffn_step(pl.program_id(1), pl.num_programs(1) - 1, x_ref, gain_ref, egain_ref, o_ref, xn_out_ref, xn_ref,
              lambda: [(wgu16_ref[s], wd16_ref[s]) for s in range(wd16_ref.shape[0])], epilogue)


def _ffn(x, gain, w_gate_up, w_down, layer, which, egain, *, epilogue, tm=1024, tf=256, tail_tiles=2):
    t, d = x.shape
    f = w_down.shape[2]
    nf = f // tf
    assert nf % tail_tiles == 0 and nf // tail_tiles >= 2
    norm_out = epilogue == "norm_out"
    once = pl.Buffered(1)
    any_spec = pl.BlockSpec(memory_space=pl.ANY)
    act_shapes = [jax.ShapeDtypeStruct((t, d), F32)] + ([jax.ShapeDtypeStruct((t, d), BF16)] if norm_out else [])
    extra_in = [jnp.zeros((t, d), BF16)] if norm_out else []

    first_rows = pl.BlockSpec((tm, d), lambda j: (0, 0), pipeline_mode=once)
    head = pl.pallas_call(
        functools.partial(_ffn_head_kernel, epilogue=epilogue),
        grid=(nf,),
        in_specs=[
            first_rows,
            pl.BlockSpec((1, d), lambda j: (0, 0)),
            pl.BlockSpec((None, None, d, tf), lambda j: (layer, which, 0, j)),
            pl.BlockSpec((None, None, d, tf), lambda j: (layer, which, 0, j + nf)),
            pl.BlockSpec((None, None, tf, d), lambda j: (layer, which, j, 0)),
            pl.BlockSpec((1, d), lambda j: (0, 0)),
        ] + [any_spec] * norm_out,
        out_specs=[first_rows] * len(act_shapes) + [
            pl.BlockSpec((None, d, 2 * tf), lambda j: (j, 0, 0)),
            pl.BlockSpec((None, tf, d), lambda j: (j, 0, 0)),
        ],
        out_shape=act_shapes + [jax.ShapeDtypeStruct((nf, d, 2 * tf), BF16),
                                jax.ShapeDtypeStruct((nf, tf, d), BF16)],
        input_output_aliases={0: 0, 6: 1} if norm_out else {0: 0},
        scratch_shapes=[pltpu.VMEM((tm, d), BF16)],
        compiler_params=_params("arbitrary"),
        name="ffn_head",
    )(x, gain.reshape(1, d), w_gate_up, w_gate_up, w_down, egain.reshape(1, d), *extra_in)
    *acts, wgu16, wd16 = head

    def rest_rows(**kw):
        return pl.BlockSpec((tm, d), lambda i, j: (i + 1, 0), **kw)

    act = tm * d
    fixed_bytes = (act * 2 + 2 * act * 4 + (2 * act * 2 if norm_out else 0)
                   + 2 * tail_tiles * 3 * d * tf * 2
                   + tm * tail_tiles * tf * (2 * 4 + 2))
    x_mode = {} if fixed_bytes + 2 * act * 4 <= VMEM_LIMIT_BYTES else {"pipeline_mode": once}

    return pl.pallas_call(
        functools.partial(_ffn_tail_kernel, epilogue=epilogue),
        grid=(t // tm - 1, nf // tail_tiles),
        in_specs=[
            rest_rows(**x_mode),
            pl.BlockSpec((1, d), lambda i, j: (0, 0)),
            pl.BlockSpec((tail_tiles, d, 2 * tf), lambda i, j: (j, 0, 0)),
            pl.BlockSpec((tail_tiles, tf, d), lambda i, j: (j, 0, 0)),
            pl.BlockSpec((1, d), lambda i, j: (0, 0)),
        ] + [any_spec] * norm_out,
        out_specs=[rest_rows()] * len(act_shapes),
        out_shape=act_shapes,
        input_output_aliases={0: 0, 5: 1} if norm_out else {0: 0},
        scratch_shapes=[pltpu.VMEM((tm, d), BF16)],
        compiler_params=_params("parallel", "arbitrary"),
        name="ffn_tail",
    )(acts[0], gain.reshape(1, d), wgu16, wd16, egain.reshape(1, d), *acts[1:])


def _out_proj_kernel(a_ref, w_ref, x_ref, o_ref, wb_ref):
    @pl.when(pl.program_id(0) == 0)
    def _():
        wb_ref[...] = w_ref[...].astype(BF16)

    o_ref[...] = x_ref[...] + _dot(a_ref[...], wb_ref[...])


def _out_proj(a, w, layer, x, *, tm=512):
    t, k = a.shape
    n = w.shape[2]
    return pl.pallas_call(
        _out_proj_kernel,
        grid=(t // tm,),
        in_specs=[
            pl.BlockSpec((tm, k), lambda i: (i, 0)),
            pl.BlockSpec((None, k, n), lambda i: (layer, 0, 0), pipeline_mode=pl.Buffered(1)),
            pl.BlockSpec((tm, n), lambda i: (i, 0)),
        ],
        out_specs=pl.BlockSpec((tm, n), lambda i: (i, 0)),
        out_shape=jax.ShapeDtypeStruct((t, n), F32),
        scratch_shapes=[pltpu.VMEM((k, n), BF16)],
        compiler_params=_params("arbitrary"),
        name="out_proj",
    )(a, w, x)


def _moba_rope_tables(seq):
    half = ROPE_DIM // 2
    inv_freq = jnp.power(jnp.float32(ROPE_THETA), -jnp.arange(half, dtype=F32) / half)
    ang = jnp.arange(seq).astype(F32)[:, None] * inv_freq[None, :]
    cos, sin = jnp.cos(ang), jnp.sin(ang)
    rest = MOBA_HEAD_DIM - ROPE_DIM
    c = jnp.concatenate([cos, cos, jnp.ones((seq, rest), F32)], axis=-1)
    a = jnp.concatenate([-sin, jnp.zeros((seq, half + rest), F32)], axis=-1)
    b = jnp.concatenate([jnp.zeros((seq, half), F32), sin, jnp.zeros((seq, rest), F32)], axis=-1)
    scale = MOBA_HEAD_DIM ** -0.5 * jnp.log2(jnp.float32(jnp.e))
    return tuple(jnp.stack([t * scale, t]) for t in (c, a, b))


def _moba_proj_kernel(xn_ref, w_ref, c_ref, a_ref, b_ref, o_ref, wb_ref, *, n_rot_tiles):
    j = pl.program_id(0)

    @pl.when(pl.program_id(1) == 0)
    def _():
        wb_ref[...] = w_ref[...].astype(BF16)

    @pl.when(j < n_rot_tiles)
    def _():
        y = _dot(xn_ref[...], wb_ref[...])
        half = ROPE_DIM // 2
        c, a, b = c_ref[...], a_ref[...], b_ref[...]
        for h in range(y.shape[1] // MOBA_HEAD_DIM):
            sl = slice(h * MOBA_HEAD_DIM, (h + 1) * MOBA_HEAD_DIM)
            yh = y[:, sl]
            up = pltpu.roll(yh, MOBA_HEAD_DIM - half, 1)
            down = pltpu.roll(yh, half, 1)
            o_ref[:, sl] = (yh * c + up * a + down * b).astype(o_ref.dtype)

    @pl.when(j >= n_rot_tiles)
    def _():
        o_ref[...] = _dot(xn_ref[...], wb_ref[...]).astype(o_ref.dtype)


def _moba_proj(xn, w_qkv, layer, seq, *, tm=1024, tn=1024):
    t, d = xn.shape
    n = w_qkv.shape[2]
    tn = min(tn, d)
    c, a, b = _moba_rope_tables(seq)
    pos_tiles = seq // tm
    q_tiles = d // tn
    tab_spec = pl.BlockSpec((None, tm, MOBA_HEAD_DIM),
                            lambda j, i: (jnp.minimum(j // q_tiles, 1), i % pos_tiles, 0))
    return pl.pallas_call(
        functools.partial(_moba_proj_kernel, n_rot_tiles=2 * q_tiles),
        grid=(n // tn, t // tm),
        in_specs=[
            pl.BlockSpec((tm, d), lambda j, i: (i, 0)),
            pl.BlockSpec((None, d, tn), lambda j, i: (layer, 0, j)),
            tab_spec, tab_spec, tab_spec,
        ],
        out_specs=pl.BlockSpec((tm, tn), lambda j, i: (i, j)),
        out_shape=jax.ShapeDtypeStruct((t, n), BF16),
        scratch_shapes=[pltpu.VMEM((d, tn), BF16)],
        compiler_params=_params("arbitrary", "arbitrary"),
        name="moba_proj",
    )(xn, w_qkv, c, a, b)


def _split_bf16(x):
    hi = x.astype(BF16)
    lo = (x - hi.astype(F32)).astype(BF16)
    return hi, lo


def _moba_attn_kernel(q_ref, k_ref, v_ref, o_ref):
    seq = q_ref.shape[0]
    blk = MOBA_BLOCK
    n_blk = seq // blk

    k16 = k_ref[...]
    vt16 = v_ref[...].astype(F32).T.astype(BF16)
    k_mean = jnp.mean(k16.astype(F32).reshape(n_blk, blk, MOBA_HEAD_DIM), axis=1)
    km_hi, km_lo = _split_bf16(k_mean)
    km16 = jnp.concatenate([km_hi, km_lo], axis=0)

    blk_row = lax.broadcasted_iota(jnp.int32, (n_blk, blk), 0)
    key_in_blk = lax.broadcasted_iota(jnp.int32, (blk, blk), 0)
    qry_in_blk = lax.broadcasted_iota(jnp.int32, (blk, blk), 1)
    causal = key_in_blk <= qry_in_blk

    for i in range(n_blk):
        q16 = q_ref[i * blk:(i + 1) * blk, :]
        n_keys = (i + 1) * blk
        st = _dot_nt(k16[:n_keys], q16)

        if i > MOBA_TOPK:
            g2 = _dot_nt(km16, q16)
            gate = g2[:n_blk] + g2[n_blk:]
            past = blk_row < i

        masked = []
        for j in range(i + 1):
            s = st[j * blk:(j + 1) * blk]
            if j == i:
                s = jnp.where(causal, s, -jnp.inf)
            elif i > MOBA_TOPK:
                gj = gate[j:j + 1, :]
                beats = ((gate > gj) | ((gate == gj) & (blk_row < j))) & past
                rank = jnp.sum(beats.astype(F32), axis=0, keepdims=True)
                s = jnp.where(rank < MOBA_TOPK, s, -jnp.inf)
            masked.append(s)

        m = masked[0]
        for s in masked[1:]:
            m = jnp.maximum(m, s)
        m = jnp.max(m, axis=0, keepdims=True)
        p = [jnp.exp2(s - m) for s in masked]
        l = p[0]
        for pj in p[1:]:
            l = l + pj
        l = jnp.sum(l, axis=0, keepdims=True)
        pt = jnp.concatenate([pj.astype(BF16) for pj in p], axis=0)
        ot = _dot(vt16[:, :n_keys], pt) / l
        o_ref[i * blk:(i + 1) * blk, :] = ot.T.astype(o_ref.dtype)


def _moba_attn(qkv, batch, seq, d):
    heads = d // MOBA_HEAD_DIM
    blk = (seq, MOBA_HEAD_DIM)
    return pl.pallas_call(
        _moba_attn_kernel,
        grid=(batch, heads),
        in_specs=[
            pl.BlockSpec(blk, lambda b, h: (b, h)),
            pl.BlockSpec(blk, lambda b, h: (b, heads + h)),
            pl.BlockSpec(blk, lambda b, h: (b, 2 * heads + h)),
        ],
        out_specs=pl.BlockSpec(blk, lambda b, h: (b, h)),
        out_shape=jax.ShapeDtypeStruct((batch * seq, d), BF16),
        compiler_params=_params("parallel", "parallel"),
        name="moba_attn",
    )(qkv, qkv, qkv)


def _ret_proj_kernel(xn_ref, w_ref, cos_ref, sin_ref, o_ref, wb_ref, *, n_q_tiles):
    j = pl.program_id(0)
    half = RET_HEAD_DIM // 2

    @pl.when(pl.program_id(1) == 0)
    def _():
        wb_ref[...] = w_ref[...].astype(BF16)

    def rotate(scale):
        y = _dot(xn_ref[...], wb_ref[...])
        cos, sin = cos_ref[...], sin_ref[...]
        for h in range(y.shape[1] // RET_HEAD_DIM):
            lo = h * RET_HEAD_DIM
            x1, x2 = y[:, lo:lo + half], y[:, lo + half:lo + RET_HEAD_DIM]
            o_ref[:, lo:lo + half] = ((x1 * cos - x2 * sin) * scale).astype(o_ref.dtype)
            o_ref[:, lo + half:lo + RET_HEAD_DIM] = ((x1 * sin + x2 * cos) * scale).astype(o_ref.dtype)

    pl.when(j < n_q_tiles)(lambda: rotate(1.0))
    pl.when((j >= n_q_tiles) & (j < 2 * n_q_tiles))(lambda: rotate(RET_HEAD_DIM ** -0.5))

    @pl.when(j >= 2 * n_q_tiles)
    def _():
        o_ref[...] = _dot(xn_ref[...], wb_ref[...]).astype(o_ref.dtype)


def _ret_proj(xn, w_in, layer, seq, *, tm=1024, tn=1024):
    t, d = xn.shape
    n = w_in.shape[2]
    tn = min(tn, d)
    half = RET_HEAD_DIM // 2
    inv_freq = jnp.power(jnp.float32(RET_ROT_BASE), -jnp.linspace(0.0, 1.0, half, dtype=F32))
    ang = jnp.arange(seq).astype(F32)[:, None] * inv_freq[None, :]
    pos_tiles = seq // tm
    tab_spec = pl.BlockSpec((tm, half), lambda j, i: (i % pos_tiles, 0))
    return pl.pallas_call(
        functools.partial(_ret_proj_kernel, n_q_tiles=d // tn),
        grid=(n // tn, t // tm),
        in_specs=[
            pl.BlockSpec((tm, d), lambda j, i: (i, 0)),
            pl.BlockSpec((None, d, tn), lambda j, i: (layer, 0, j)),
            tab_spec, tab_spec,
        ],
        out_specs=pl.BlockSpec((tm, tn), lambda j, i: (i, j)),
        out_shape=jax.ShapeDtypeStruct((t, n), BF16),
        scratch_shapes=[pltpu.VMEM((d, tn), BF16)],
        compiler_params=_params("arbitrary", "arbitrary"),
        name="ret_proj",
    )(xn, w_in, jnp.cos(ang), jnp.sin(ang))


def _ret_kernel(q_ref, k_ref, v_ref, g_ref, gn_ref, dmask_ref, qdec_ref, kdec_ref, cdec_ref, o_ref):
    seq = q_ref.shape[0]
    c = RET_CHUNK
    n_chunks = seq // c
    dmask = dmask_ref[...]
    qdec = qdec_ref[...]
    kdec = kdec_ref[...]
    cdec = cdec_ref[...]
    gn = gn_ref[...]

    state = None
    for n in range(n_chunks):
        rows = slice(n * c, (n + 1) * c)
        qc, kc, vc = q_ref[rows, :], k_ref[rows, :], v_ref[rows, :]
        s = _dot_nt(qc, kc) * dmask
        out = _dot(s.astype(BF16), vc)
        if state is not None:
            out = out + _dot(qc, state.astype(BF16)) * qdec
        if n + 1 < n_chunks:
            kt = (kc.astype(F32) * kdec).T.astype(BF16)
            upd = _dot(kt, vc)
            state = upd if state is None else state * cdec + upd
        out = out * lax.rsqrt(jnp.mean(out * out, axis=-1, keepdims=True) + RMS_EPS) * gn
        o_ref[rows, :] = (_silu(g_ref[rows, :].astype(F32)) * out).astype(o_ref.dtype)


def _retention(proj, gn_gain, layer, batch, seq, d):
    heads = d // RET_HEAD_DIM
    c = RET_CHUNK
    log_gamma = jnp.log1p(-jnp.power(2.0, -5.0 - jnp.arange(heads, dtype=F32)))
    n = jnp.arange(c, dtype=F32)
    diff = n[:, None] - n[None, :]
    dmask = jnp.exp(jnp.where((diff >= 0)[None], diff[None] * log_gamma[:, None, None], -jnp.inf))
    qdec = jnp.exp((n[None, :] + 1.0) * log_gamma[:, None])[:, :, None]
    kdec = jnp.exp((c - 1.0 - n[None, :]) * log_gamma[:, None])[:, :, None]
    cdec = jnp.broadcast_to(jnp.exp(c * log_gamma)[:, None, None], (heads, 1, RET_HEAD_DIM))
    blk = (seq, RET_HEAD_DIM)
    return pl.pallas_call(
        _ret_kernel,
        grid=(batch, heads),
        in_specs=[
            pl.BlockSpec(blk, lambda b, h: (b, h)),
            pl.BlockSpec(blk, lambda b, h: (b, heads + h)),
            pl.BlockSpec(blk, lambda b, h: (b, 2 * heads + h)),
            pl.BlockSpec(blk, lambda b, h: (b, 3 * heads + h)),
            pl.BlockSpec((None, 1, RET_HEAD_DIM), lambda b, h: (layer, 0, h)),
            pl.BlockSpec((None, c, c), lambda b, h: (h, 0, 0)),
            pl.BlockSpec((None, c, 1), lambda b, h: (h, 0, 0)),
            pl.BlockSpec((None, c, 1), lambda b, h: (h, 0, 0)),
            pl.BlockSpec((None, 1, RET_HEAD_DIM), lambda b, h: (h, 0, 0)),
        ],
        out_specs=pl.BlockSpec(blk, lambda b, h: (b, h)),
        out_shape=jax.ShapeDtypeStruct((batch * seq, d), BF16),
        compiler_params=_params("parallel", "parallel"),
        name="retention",
    )(proj, proj, proj, proj, gn_gain.reshape(gn_gain.shape[0], 1, d), dmask, qdec, kdec, cdec)


def kernel(x, norm_gain, ffn_w_gate_up, ffn_w_down, moba_w_qkv, moba_w_o, ret_w_in, ret_w_o, ret_gn_gain, final_norm):
    batch, seq, d = x.shape
    depth = norm_gain.shape[0]
    h = x.reshape(batch * seq, d)
    for i in range(depth):
        g = norm_gain[i]
        h, hn = _ffn(h, g[0], ffn_w_gate_up, ffn_w_down, i, 0, g[1], epilogue="norm_out")
        if i % 2 == 0:
            qkv = _moba_proj(hn, moba_w_qkv, i // 2, seq)
            h = _out_proj(_moba_attn(qkv, batch, seq, d), moba_w_o, i // 2, h)
        else:
            proj = _ret_proj(hn, ret_w_in, i // 2, seq)
            h = _out_proj(_retention(proj, ret_gn_gain, i // 2, batch, seq, d), ret_w_o, i // 2, h)
        h = _ffn(h, g[2], ffn_w_gate_up, ffn_w_down, i, 1, final_norm,
                 epilogue="final_norm" if i == depth - 1 else "none")[0]
    return h.reshape(batch, seq, d)
```

```python
import functools

import jax
import jax.numpy as jnp
from jax import lax
from jax.experimental import pallas as pl
from jax.experimental.pallas import tpu as pltpu

FFN_RES = 0.5
RMS_EPS = 1e-6

MOBA_HEAD_DIM = 128
MOBA_BLOCK = 256
MOBA_TOPK = 3
ROPE_THETA = 500000.0
ROPE_DIM = MOBA_HEAD_DIM // 4

RET_HEAD_DIM = 256
RET_CHUNK = 256
RET_ROT_BASE = 10000.0

V7X_VMEM_BYTES = 64 * 1024 * 1024
BF16_SUBLANES = 16
VMEM_LIMIT_BYTES = V7X_VMEM_BYTES - 8 * 1024 * 1024

F32 = jnp.float32
BF16 = jnp.bfloat16


def _params(*semantics):
    return pltpu.CompilerParams(dimension_semantics=semantics, vmem_limit_bytes=VMEM_LIMIT_BYTES)


def _rms_norm(x, gain):
    return x * lax.rsqrt(jnp.mean(x * x, axis=-1, keepdims=True) + RMS_EPS) * gain


def _dot(a, b):
    return jnp.dot(a, b, preferred_element_type=F32)


def _dot_nt(a, b):
    return lax.dot_general(a, b, (((1,), (1,)), ((), ())), preferred_element_type=F32)


def _silu(g):
    return g * jax.nn.sigmoid(g)


def _ffn_step(j, last_j, x_ref, gain_ref, egain_ref, o_ref, xn_out_ref, xn_ref, weights, epilogue):
    def step(first, last):
        if first:
            xn = _rms_norm(x_ref[...], gain_ref[...]).astype(BF16)
            xn_ref[...] = xn
        else:
            xn = xn_ref[...]
        tiles = weights()
        hidden = []
        for wgu, wd in tiles:
            tf = wd.shape[0]
            gu = _dot(xn, wgu)
            hidden.append((_silu(gu[:, :tf]) * gu[:, tf:]).astype(BF16))
        acc = _dot(jnp.concatenate(hidden, axis=1), jnp.concatenate([wd for _, wd in tiles], axis=0))
        if not first:
            acc = o_ref[...] + acc
        if last:
            y = x_ref[...] + FFN_RES * acc
            if epilogue == "final_norm":
                y = _rms_norm(y, egain_ref[...])
            elif epilogue == "norm_out":
                xn_out_ref[...] = _rms_norm(y, egain_ref[...]).astype(BF16)
            o_ref[...] = y
        else:
            o_ref[...] = acc

    pl.when(j == 0)(lambda: step(True, False))
    pl.when((j > 0) & (j < last_j))(lambda: step(False, False))
    pl.when(j == last_j)(lambda: step(False, True))


def _ffn_head_kernel(x_ref, gain_ref, wg_ref, wu_ref, wd_ref, egain_ref, *rest, epilogue):
    if epilogue == "norm_out":
        _, o_ref, xn_out_ref, wgu16_ref, wd16_ref, xn_ref = rest
    else:
        o_ref, wgu16_ref, wd16_ref, xn_ref = rest
        xn_out_ref = None
    tf = wd_ref.shape[0]

    def weights():
        wgu16_ref[:, :tf] = wg_ref[...].astype(BF16)
        wgu16_ref[:, tf:] = wu_ref[...].astype(BF16)
        wd16_ref[...] = wd_ref[...].astype(BF16)
        return [(wgu16_ref[...], wd16_ref[...])]

    _ffn_step(pl.program_id(0), pl.num_programs(0) - 1, x_ref, gain_ref, egain_ref, o_ref, xn_out_ref, xn_ref,
              weights, epilogue)


def _ffn_tail_kernel(x_ref, gain_ref, wgu16_ref, wd16_ref, egain_ref, *rest, epilogue):
    if epilogue == "norm_out":
        _, o_ref, xn_out_ref = rest
        xn_ref = xn_out_ref
    else:
        o_ref, xn_ref = rest
        xn_out_ref = None
    _ffn_step(pl.program_id(1), pl.num_programs(1) - 1, x_ref, gain_ref, egain_ref, o_ref, xn_out_ref, xn_ref,
              lambda: [(wgu16_ref[s], wd16_ref[s]) for s in range(wd16_ref.shape[0])], epilogue)


def _ffn(x, gain, w_gate_up, w_down, layer, which, egain, *, epilogue, tm=1024, tf=256, tail_tiles=2):
    t, d = x.shape
    f = w_down.shape[2]
    nf = f // tf
    assert nf % tail_tiles == 0 and nf // tail_tiles >= 2
    norm_out = epilogue == "norm_out"
    once = pl.Buffered(1)
    any_spec = pl.BlockSpec(memory_space=pl.ANY)
    act_shapes = [jax.ShapeDtypeStruct((t, d), F32)] + ([jax.ShapeDtypeStruct((t, d), BF16)] if norm_out else [])
    extra_in = [jnp.zeros((t, d), BF16)] if norm_out else []

    first_rows = pl.BlockSpec((tm, d), lambda j: (0, 0), pipeline_mode=once)
    head = pl.pallas_call(
        functools.partial(_ffn_head_kernel, epilogue=epilogue),
        grid=(nf,),
        in_specs=[
            first_rows,
            pl.BlockSpec((1, d), lambda j: (0, 0)),
            pl.BlockSpec((None, None, d, tf), lambda j: (layer, which, 0, j)),
            pl.BlockSpec((None, None, d, tf), lambda j: (layer, which, 0, j + nf)),
            pl.BlockSpec((None, None, tf, d), lambda j: (layer, which, j, 0)),
            pl.BlockSpec((1, d), lambda j: (0, 0)),
        ] + [any_spec] * norm_out,
        out_specs=[first_rows] * len(act_shapes) + [
            pl.BlockSpec((None, d, 2 * tf), lambda j: (j, 0, 0)),
            pl.BlockSpec((None, tf, d), lambda j: (j, 0, 0)),
        ],
        out_shape=act_shapes + [jax.ShapeDtypeStruct((nf, d, 2 * tf), BF16),
                                jax.ShapeDtypeStruct((nf, tf, d), BF16)],
        input_output_aliases={0: 0, 6: 1} if norm_out else {0: 0},
        scratch_shapes=[pltpu.VMEM((tm, d), BF16)],
        compiler_params=_params("arbitrary"),
        name="ffn_head",
    )(x, gain.reshape(1, d), w_gate_up, w_gate_up, w_down, egain.reshape(1, d), *extra_in)
    *acts, wgu16, wd16 = head

    def rest_rows(**kw):
        return pl.BlockSpec((tm, d), lambda i, j: (i + 1, 0), **kw)

    act = tm * d
    fixed_bytes = (act * 2 + 2 * act * 4
                   + 2 * tail_tiles * 3 * d * tf * 2
                   + tm * tf * (2 * 4 + 2))
    x_mode = {} if fixed_bytes + 2 * act * 4 <= VMEM_LIMIT_BYTES else {"pipeline_mode": once}

    return pl.pallas_call(
        functools.partial(_ffn_tail_kernel, epilogue=epilogue),
        grid=(t // tm - 1, nf // tail_tiles),
        in_specs=[
            rest_rows(**x_mode),
            pl.BlockSpec((1, d), lambda i, j: (0, 0)),
            pl.BlockSpec((tail_tiles, d, 2 * tf), lambda i, j: (j, 0, 0)),
            pl.BlockSpec((tail_tiles, tf, d), lambda i, j: (j, 0, 0)),
            pl.BlockSpec((1, d), lambda i, j: (0, 0)),
        ] + [any_spec] * norm_out,
        out_specs=[rest_rows()] + [rest_rows(pipeline_mode=once)] * norm_out,
        out_shape=act_shapes,
        input_output_aliases={0: 0, 5: 1} if norm_out else {0: 0},
        scratch_shapes=[] if norm_out else [pltpu.VMEM((tm, d), BF16)],
        compiler_params=_params("parallel", "arbitrary"),
        name="ffn_tail",
    )(acts[0], gain.reshape(1, d), wgu16, wd16, egain.reshape(1, d), *acts[1:])


def _out_proj_kernel(a_ref, w_ref, x_ref, o_ref, wb_ref):
    @pl.when(pl.program_id(0) == 0)
    def _():
        wb_ref[...] = w_ref[...].astype(BF16)

    o_ref[...] = x_ref[...] + _dot(a_ref[...], wb_ref[...])


def _out_proj(a, w, layer, x, *, tm=512):
    t, k = a.shape
    n = w.shape[2]
    return pl.pallas_call(
        _out_proj_kernel,
        grid=(t // tm,),
        in_specs=[
            pl.BlockSpec((tm, k), lambda i: (i, 0)),
            pl.BlockSpec((None, k, n), lambda i: (layer, 0, 0), pipeline_mode=pl.Buffered(1)),
            pl.BlockSpec((tm, n), lambda i: (i, 0)),
        ],
        out_specs=pl.BlockSpec((tm, n), lambda i: (i, 0)),
        out_shape=jax.ShapeDtypeStruct((t, n), F32),
        scratch_shapes=[pltpu.VMEM((k, n), BF16)],
        compiler_params=_params("arbitrary"),
        name="out_proj",
    )(a, w, x)


def _moba_rope_tables(seq):
    half = ROPE_DIM // 2
    inv_freq = jnp.power(jnp.float32(ROPE_THETA), -jnp.arange(half, dtype=F32) / half)
    ang = jnp.arange(seq).astype(F32)[:, None] * inv_freq[None, :]
    cos, sin = jnp.cos(ang), jnp.sin(ang)
    rest = MOBA_HEAD_DIM - ROPE_DIM
    c = jnp.concatenate([cos, cos, jnp.ones((seq, rest), F32)], axis=-1)
    a = jnp.concatenate([-sin, jnp.zeros((seq, half + rest), F32)], axis=-1)
    b = jnp.concatenate([jnp.zeros((seq, half), F32), sin, jnp.zeros((seq, rest), F32)], axis=-1)
    scale = MOBA_HEAD_DIM ** -0.5 * jnp.log2(jnp.float32(jnp.e))
    return tuple(jnp.stack([t * scale, t]) for t in (c, a, b))


def _moba_proj_kernel(xn_ref, w_ref, c_ref, a_ref, b_ref, o_ref, wb_ref, *, n_rot_tiles):
    j = pl.program_id(0)

    @pl.when(pl.program_id(1) == 0)
    def _():
        wb_ref[...] = w_ref[...].astype(BF16)

    @pl.when(j < n_rot_tiles)
    def _():
        y = _dot(xn_ref[...], wb_ref[...])
        half = ROPE_DIM // 2
        c, a, b = c_ref[...], a_ref[...], b_ref[...]
        for h in range(y.shape[1] // MOBA_HEAD_DIM):
            sl = slice(h * MOBA_HEAD_DIM, (h + 1) * MOBA_HEAD_DIM)
            yh = y[:, sl]
            up = pltpu.roll(yh, MOBA_HEAD_DIM - half, 1)
            down = pltpu.roll(yh, half, 1)
            o_ref[:, sl] = (yh * c + up * a + down * b).astype(o_ref.dtype)

    @pl.when(j >= n_rot_tiles)
    def _():
        o_ref[...] = _dot(xn_ref[...], wb_ref[...]).astype(o_ref.dtype)


def _moba_proj(xn, w_qkv, layer, seq, *, tm=1024, tn=1024):
    t, d = xn.shape
    n = w_qkv.shape[2]
    tn = min(tn, d)
    c, a, b = _moba_rope_tables(seq)
    pos_tiles = seq // tm
    q_tiles = d // tn
    tab_spec = pl.BlockSpec((None, tm, MOBA_HEAD_DIM),
                            lambda j, i: (jnp.minimum(j // q_tiles, 1), i % pos_tiles, 0))
    return pl.pallas_call(
        functools.partial(_moba_proj_kernel, n_rot_tiles=2 * q_tiles),
        grid=(n // tn, t // tm),
        in_specs=[
            pl.BlockSpec((tm, d), lambda j, i: (i, 0)),
            pl.BlockSpec((None, d, tn), lambda j, i: (layer, 0, j)),
            tab_spec, tab_spec, tab_spec,
        ],
        out_specs=pl.BlockSpec((tm, tn), lambda j, i: (i, j)),
        out_shape=jax.ShapeDtypeStruct((t, n), BF16),
        scratch_shapes=[pltpu.VMEM((d, tn), BF16)],
        compiler_params=_params("arbitrary", "arbitrary"),
        name="moba_proj",
    )(xn, w_qkv, c, a, b)


def _split_bf16(x):
    hi = x.astype(BF16)
    lo = (x - hi.astype(F32)).astype(BF16)
    return hi, lo


def _moba_attn_kernel(q_ref, k_ref, v_ref, o_ref):
    seq = q_ref.shape[0]
    blk = MOBA_BLOCK
    n_blk = seq // blk

    k16 = k_ref[...]
    vt16 = v_ref[...].astype(F32).T.astype(BF16)
    vt_ext = jnp.concatenate([vt16, jnp.ones((BF16_SUBLANES, seq), BF16)], axis=0)
    k_mean = jnp.mean(k16.astype(F32).reshape(n_blk, blk, MOBA_HEAD_DIM), axis=1)
    km_hi, km_lo = _split_bf16(k_mean)
    km16 = jnp.concatenate([km_hi, km_lo], axis=0)

    blk_row = lax.broadcasted_iota(jnp.int32, (n_blk, blk), 0)
    key_in_blk = lax.broadcasted_iota(jnp.int32, (blk, blk), 0)
    qry_in_blk = lax.broadcasted_iota(jnp.int32, (blk, blk), 1)
    causal = key_in_blk <= qry_in_blk

    for i in range(n_blk):
        q16 = q_ref[i * blk:(i + 1) * blk, :]
        n_keys = (i + 1) * blk
        st = _dot_nt(k16[:n_keys], q16)

        if i > MOBA_TOPK:
            g2 = _dot_nt(km16, q16)
            gate = g2[:n_blk] + g2[n_blk:]
            past = blk_row < i

        m = jnp.full((1, blk), jnp.finfo(F32).min, F32)
        acc = None
        for j in range(i + 1):
            s = st[j * blk:(j + 1) * blk]
            if j == i:
                s = jnp.where(causal, s, -jnp.inf)
                m_new = jnp.maximum(m, jnp.max(s, axis=0, keepdims=True))
                shift = m_new
            elif i > MOBA_TOPK:
                gj = gate[j:j + 1, :]
                beats = ((gate > gj) | ((gate == gj) & (blk_row < j))) & past
                keep = jnp.sum(beats.astype(F32), axis=0, keepdims=True) < MOBA_TOPK
                m_new = jnp.maximum(m, jnp.where(keep, jnp.max(s, axis=0, keepdims=True), -jnp.inf))
                shift = jnp.where(keep, m_new, jnp.inf)
            else:
                m_new = jnp.maximum(m, jnp.max(s, axis=0, keepdims=True))
                shift = m_new
            p = jnp.exp2(s - shift).astype(BF16)
            acc_j = _dot(vt_ext[:, j * blk:(j + 1) * blk], p)
            acc = acc_j if j == 0 else jnp.exp2(m - m_new) * acc + acc_j
            m = m_new
        out = acc[:MOBA_HEAD_DIM] / acc[MOBA_HEAD_DIM:MOBA_HEAD_DIM + 1]
        o_ref[i * blk:(i + 1) * blk, :] = out.T.astype(o_ref.dtype)


def _moba_attn(qkv, batch, seq, d):
    heads = d // MOBA_HEAD_DIM
    blk = (seq, MOBA_HEAD_DIM)
    return pl.pallas_call(
        _moba_attn_kernel,
        grid=(batch, heads),
        in_specs=[
            pl.BlockSpec(blk, lambda b, h: (b, h)),
            pl.BlockSpec(blk, lambda b, h: (b, heads + h)),
            pl.BlockSpec(blk, lambda b, h: (b, 2 * heads + h)),
        ],
        out_specs=pl.BlockSpec(blk, lambda b, h: (b, h)),
        out_shape=jax.ShapeDtypeStruct((batch * seq, d), BF16),
        compiler_params=_params("parallel", "parallel"),
        name="moba_attn",
    )(qkv, qkv, qkv)


def _ret_proj_kernel(xn_ref, w_ref, cos_ref, sin_ref, o_ref, wb_ref, *, n_q_tiles):
    j = pl.program_id(0)
    half = RET_HEAD_DIM // 2

    @pl.when(pl.program_id(1) == 0)
    def _():
        wb_ref[...] = w_ref[...].astype(BF16)

    def rotate(scale):
        y = _dot(xn_ref[...], wb_ref[...])
        cos, sin = cos_ref[...], sin_ref[...]
        for h in range(y.shape[1] // RET_HEAD_DIM):
            lo = h * RET_HEAD_DIM
            x1, x2 = y[:, lo:lo + half], y[:, lo + half:lo + RET_HEAD_DIM]
            o_ref[:, lo:lo + half] = ((x1 * cos - x2 * sin) * scale).astype(o_ref.dtype)
            o_ref[:, lo + half:lo + RET_HEAD_DIM] = ((x1 * sin + x2 * cos) * scale).astype(o_ref.dtype)

    pl.when(j < n_q_tiles)(lambda: rotate(1.0))
    pl.when((j >= n_q_tiles) & (j < 2 * n_q_tiles))(lambda: rotate(RET_HEAD_DIM ** -0.5))

    @pl.when(j >= 2 * n_q_tiles)
    def _():
        o_ref[...] = _dot(xn_ref[...], wb_ref[...]).astype(o_ref.dtype)


def _ret_proj(xn, w_in, layer, seq, *, tm=1024, tn=1024):
    t, d = xn.shape
    n = w_in.shape[2]
    tn = min(tn, d)
    half = RET_HEAD_DIM // 2
    inv_freq = jnp.power(jnp.float32(RET_ROT_BASE), -jnp.linspace(0.0, 1.0, half, dtype=F32))
    ang = jnp.arange(seq).astype(F32)[:, None] * inv_freq[None, :]
    pos_tiles = seq // tm
    tab_spec = pl.BlockSpec((tm, half), lambda j, i: (i % pos_tiles, 0))
    return pl.pallas_call(
        functools.partial(_ret_proj_kernel, n_q_tiles=d // tn),
        grid=(n // tn, t // tm),
        in_specs=[
            pl.BlockSpec((tm, d), lambda j, i: (i, 0)),
            pl.BlockSpec((None, d, tn), lambda j, i: (layer, 0, j)),
            tab_spec, tab_spec,
        ],
        out_specs=pl.BlockSpec((tm, tn), lambda j, i: (i, j)),
        out_shape=jax.ShapeDtypeStruct((t, n), BF16),
        scratch_shapes=[pltpu.VMEM((d, tn), BF16)],
        compiler_params=_params("arbitrary", "arbitrary"),
        name="ret_proj",
    )(xn, w_in, jnp.cos(ang), jnp.sin(ang))


def _ret_kernel(q_ref, k_ref, v_ref, g_ref, gn_ref, dmask_ref, qdec_ref, kdec_ref, cdec_ref, o_ref):
    seq = q_ref.shape[0]
    c = RET_CHUNK
    n_chunks = seq // c
    dmask = dmask_ref[...]
    qdec = qdec_ref[...]
    kdec = kdec_ref[...]
    cdec = cdec_ref[...]
    gn = gn_ref[...]

    state = None
    for n in range(n_chunks):
        rows = slice(n * c, (n + 1) * c)
        qc, kc, vc = q_ref[rows, :], k_ref[rows, :], v_ref[rows, :]
        s = _dot_nt(qc, kc) * dmask
        out = _dot(s.astype(BF16), vc)
        if state is not None:
            out = out + _dot(qc, state.astype(BF16)) * qdec
        if n + 1 < n_chunks:
            kt = (kc.astype(F32) * kdec).T.astype(BF16)
            upd = _dot(kt, vc)
            state = upd if state is None else state * cdec + upd
        out = out * lax.rsqrt(jnp.mean(out * out, axis=-1, keepdims=True) + RMS_EPS) * gn
        o_ref[rows, :] = (_silu(g_ref[rows, :].astype(F32)) * out).astype(o_ref.dtype)


def _retention(proj, gn_gain, layer, batch, seq, d):
    heads = d // RET_HEAD_DIM
    c = RET_CHUNK
    log_gamma = jnp.log1p(-jnp.power(2.0, -5.0 - jnp.arange(heads, dtype=F32)))
    n = jnp.arange(c, dtype=F32)
    diff = n[:, None] - n[None, :]
    dmask = jnp.exp(jnp.where((diff >= 0)[None], diff[None] * log_gamma[:, None, None], -jnp.inf))
    qdec = jnp.exp((n[None, :] + 1.0) * log_gamma[:, None])[:, :, None]
    kdec = jnp.exp((c - 1.0 - n[None, :]) * log_gamma[:, None])[:, :, None]
    cdec = jnp.broadcast_to(jnp.exp(c * log_gamma)[:, None, None], (heads, 1, RET_HEAD_DIM))
    blk = (seq, RET_HEAD_DIM)
    return pl.pallas_call(
        _ret_kernel,
        grid=(batch, heads),
        in_specs=[
            pl.BlockSpec(blk, lambda b, h: (b, h)),
            pl.BlockSpec(blk, lambda b, h: (b, heads + h)),
            pl.BlockSpec(blk, lambda b, h: (b, 2 * heads + h)),
            pl.BlockSpec(blk, lambda b, h: (b, 3 * heads + h)),
            pl.BlockSpec((None, 1, RET_HEAD_DIM), lambda b, h: (layer, 0, h)),
            pl.BlockSpec((None, c, c), lambda b, h: (h, 0, 0)),
            pl.BlockSpec((None, c, 1), lambda b, h: (h, 0, 0)),
            pl.BlockSpec((None, c, 1), lambda b, h: (h, 0, 0)),
            pl.BlockSpec((None, 1, RET_HEAD_DIM), lambda b, h: (h, 0, 0)),
        ],
        out_specs=pl.BlockSpec(blk, lambda b, h: (b, h)),
        out_shape=jax.ShapeDtypeStruct((batch * seq, d), BF16),
        compiler_params=_params("parallel", "parallel"),
        name="retention",
    )(proj, proj, proj, proj, gn_gain.reshape(gn_gain.shape[0], 1, d), dmask, qdec, kdec, cdec)


def kernel(x, norm_gain, ffn_w_gate_up, ffn_w_down, moba_w_qkv, moba_w_o, ret_w_in, ret_w_o, ret_gn_gain, final_norm):
    batch, seq, d = x.shape
    depth = norm_gain.shape[0]
    h = x.reshape(batch * seq, d)
    for i in range(depth):
        g = norm_gain[i]
        h, hn = _ffn(h, g[0], ffn_w_gate_up, ffn_w_down, i, 0, g[1], epilogue="norm_out")
        if i % 2 == 0:
            qkv = _moba_proj(hn, moba_w_qkv, i // 2, seq)
            h = _out_proj(_moba_attn(qkv, batch, seq, d), moba_w_o, i // 2, h)
        else:
            proj = _ret_proj(hn, ret_w_in, i // 2, seq)
            h = _out_proj(_retention(proj, ret_gn_gain, i // 2, batch, seq, d), ret_w_o, i // 2, h)
        h = _ffn(h, g[2], ffn_w_gate_up, ffn_w_down, i, 1, final_norm,
                 epilogue="final_norm" if i == depth - 1 else "none")[0]
    return h.reshape(batch, seq, d)
```

```python
import functools

import jax
import jax.numpy as jnp
from jax import lax
from jax.experimental import pallas as pl
from jax.experimental.pallas import tpu as pltpu

FFN_RES = 0.5
RMS_EPS = 1e-6

MOBA_HEAD_DIM = 128
MOBA_BLOCK = 256
MOBA_TOPK = 3
ROPE_THETA = 500000.0
ROPE_DIM = MOBA_HEAD_DIM // 4

RET_HEAD_DIM = 256
RET_CHUNK = 256
RET_ROT_BASE = 10000.0

V7X_VMEM_BYTES = 64 * 1024 * 1024
BF16_SUBLANES = 16
VMEM_LIMIT_BYTES = V7X_VMEM_BYTES - 8 * 1024 * 1024

F32 = jnp.float32
BF16 = jnp.bfloat16


def _params(*semantics):
    return pltpu.CompilerParams(dimension_semantics=semantics, vmem_limit_bytes=VMEM_LIMIT_BYTES)


def _rms_norm(x, gain):
    return x * lax.rsqrt(jnp.mean(x * x, axis=-1, keepdims=True) + RMS_EPS) * gain


def _dot(a, b):
    return jnp.dot(a, b, preferred_element_type=F32)


def _dot_nt(a, b):
    return lax.dot_general(a, b, (((1,), (1,)), ((), ())), preferred_element_type=F32)


def _silu(g):
    return g * jax.nn.sigmoid(g)


def _ffn_step(j, last_j, x_ref, gain_ref, egain_ref, o_ref, xn_out_ref, xn_ref, weights, epilogue):
    def step(first, last):
        if first:
            xn = _rms_norm(x_ref[...], gain_ref[...]).astype(BF16)
            xn_ref[...] = xn
        else:
            xn = xn_ref[...]
        tiles = weights()
        hidden = []
        for wgu, wd in tiles:
            tf = wd.shape[0]
            gu = _dot(xn, wgu)
            hidden.append((_silu(gu[:, :tf]) * gu[:, tf:]).astype(BF16))
        acc = _dot(jnp.concatenate(hidden, axis=1), jnp.concatenate([wd for _, wd in tiles], axis=0))
        if not first:
            acc = o_ref[...] + acc
        if last:
            y = x_ref[...] + FFN_RES * acc
            if epilogue == "final_norm":
                y = _rms_norm(y, egain_ref[...])
            elif epilogue == "norm_out":
                xn_out_ref[...] = _rms_norm(y, egain_ref[...]).astype(BF16)
            o_ref[...] = y
        else:
            o_ref[...] = acc

    pl.when(j == 0)(lambda: step(True, False))
    pl.when((j > 0) & (j < last_j))(lambda: step(False, False))
    pl.when(j == last_j)(lambda: step(False, True))


def _ffn_head_kernel(x_ref, gain_ref, wg_ref, wu_ref, wd_ref, egain_ref, *rest, epilogue):
    if epilogue == "norm_out":
        _, o_ref, xn_out_ref, wgu16_ref, wd16_ref, xn_ref = rest
    else:
        o_ref, wgu16_ref, wd16_ref, xn_ref = rest
        xn_out_ref = None
    tf = wd_ref.shape[0]

    def weights():
        wgu16_ref[:, :tf] = wg_ref[...].astype(BF16)
        wgu16_ref[:, tf:] = wu_ref[...].astype(BF16)
        wd16_ref[...] = wd_ref[...].astype(BF16)
        return [(wgu16_ref[...], wd16_ref[...])]

    _ffn_step(pl.program_id(0), pl.num_programs(0) - 1, x_ref, gain_ref, egain_ref, o_ref, xn_out_ref, xn_ref,
              weights, epilogue)


def _ffn_tail_kernel(x_ref, gain_ref, wgu16_ref, wd16_ref, egain_ref, *rest, epilogue):
    if epilogue == "norm_out":
        _, o_ref, xn_out_ref = rest
        xn_ref = xn_out_ref
    else:
        o_ref, xn_ref = rest
        xn_out_ref = None
    _ffn_step(pl.program_id(1), pl.num_programs(1) - 1, x_ref, gain_ref, egain_ref, o_ref, xn_out_ref, xn_ref,
              lambda: [(wgu16_ref[s], wd16_ref[s]) for s in range(wd16_ref.shape[0])], epilogue)


def _ffn(x, gain, w_gate_up, w_down, layer, which, egain, *, epilogue, tm=1024, tf=256, tail_tiles=2):
    t, d = x.shape
    f = w_down.shape[2]
    nf = f // tf
    assert nf % tail_tiles == 0 and nf // tail_tiles >= 2
    norm_out = epilogue == "norm_out"
    once = pl.Buffered(1)
    any_spec = pl.BlockSpec(memory_space=pl.ANY)
    act_shapes = [jax.ShapeDtypeStruct((t, d), F32)] + ([jax.ShapeDtypeStruct((t, d), BF16)] if norm_out else [])
    extra_in = [jnp.zeros((t, d), BF16)] if norm_out else []

    first_rows = pl.BlockSpec((tm, d), lambda j: (0, 0), pipeline_mode=once)
    head = pl.pallas_call(
        functools.partial(_ffn_head_kernel, epilogue=epilogue),
        grid=(nf,),
        in_specs=[
            first_rows,
            pl.BlockSpec((1, d), lambda j: (0, 0)),
            pl.BlockSpec((None, None, d, tf), lambda j: (layer, which, 0, j)),
            pl.BlockSpec((None, None, d, tf), lambda j: (layer, which, 0, j + nf)),
            pl.BlockSpec((None, None, tf, d), lambda j: (layer, which, j, 0)),
            pl.BlockSpec((1, d), lambda j: (0, 0)),
        ] + [any_spec] * norm_out,
        out_specs=[first_rows] * len(act_shapes) + [
            pl.BlockSpec((None, d, 2 * tf), lambda j: (j, 0, 0)),
            pl.BlockSpec((None, tf, d), lambda j: (j, 0, 0)),
        ],
        out_shape=act_shapes + [jax.ShapeDtypeStruct((nf, d, 2 * tf), BF16),
                                jax.ShapeDtypeStruct((nf, tf, d), BF16)],
        input_output_aliases={0: 0, 6: 1} if norm_out else {0: 0},
        scratch_shapes=[pltpu.VMEM((tm, d), BF16)],
        compiler_params=_params("arbitrary"),
        name="ffn_head",
    )(x, gain.reshape(1, d), w_gate_up, w_gate_up, w_down, egain.reshape(1, d), *extra_in)
    *acts, wgu16, wd16 = head

    def rest_rows(**kw):
        return pl.BlockSpec((tm, d), lambda i, j: (i + 1, 0), **kw)

    act = tm * d
    fixed_bytes = (act * 2 + 2 * act * 4
                   + 2 * tail_tiles * 3 * d * tf * 2
                   + tm * tf * (2 * 4 + 2))
    x_mode = {} if fixed_bytes + 2 * act * 4 <= VMEM_LIMIT_BYTES else {"pipeline_mode": once}

    return pl.pallas_call(
        functools.partial(_ffn_tail_kernel, epilogue=epilogue),
        grid=(t // tm - 1, nf // tail_tiles),
        in_specs=[
            rest_rows(**x_mode),
            pl.BlockSpec((1, d), lambda i, j: (0, 0)),
            pl.BlockSpec((tail_tiles, d, 2 * tf), lambda i, j: (j, 0, 0)),
            pl.BlockSpec((tail_tiles, tf, d), lambda i, j: (j, 0, 0)),
            pl.BlockSpec((1, d), lambda i, j: (0, 0)),
        ] + [any_spec] * norm_out,
        out_specs=[rest_rows()] + [rest_rows(pipeline_mode=once)] * norm_out,
        out_shape=act_shapes,
        input_output_aliases={0: 0, 5: 1} if norm_out else {0: 0},
        scratch_shapes=[] if norm_out else [pltpu.VMEM((tm, d), BF16)],
        compiler_params=_params("parallel", "arbitrary"),
        name="ffn_tail",
    )(acts[0], gain.reshape(1, d), wgu16, wd16, egain.reshape(1, d), *acts[1:])


def _out_proj_kernel(a_ref, w_ref, x_ref, o_ref, wb_ref):
    @pl.when(pl.program_id(0) == 0)
    def _():
        wb_ref[...] = w_ref[...].astype(BF16)

    o_ref[...] = x_ref[...] + _dot(a_ref[...], wb_ref[...])


def _out_proj(a, w, layer, x, *, tm=512):
    t, k = a.shape
    n = w.shape[2]
    return pl.pallas_call(
        _out_proj_kernel,
        grid=(t // tm,),
        in_specs=[
            pl.BlockSpec((tm, k), lambda i: (i, 0)),
            pl.BlockSpec((None, k, n), lambda i: (layer, 0, 0), pipeline_mode=pl.Buffered(1)),
            pl.BlockSpec((tm, n), lambda i: (i, 0)),
        ],
        out_specs=pl.BlockSpec((tm, n), lambda i: (i, 0)),
        out_shape=jax.ShapeDtypeStruct((t, n), F32),
        scratch_shapes=[pltpu.VMEM((k, n), BF16)],
        compiler_params=_params("arbitrary"),
        name="out_proj",
    )(a, w, x)


def _moba_rope_tables(seq):
    half = ROPE_DIM // 2
    inv_freq = jnp.power(jnp.float32(ROPE_THETA), -jnp.arange(half, dtype=F32) / half)
    ang = jnp.arange(seq).astype(F32)[:, None] * inv_freq[None, :]
    cos, sin = jnp.cos(ang), jnp.sin(ang)
    rest = MOBA_HEAD_DIM - ROPE_DIM
    c = jnp.concatenate([cos, cos, jnp.ones((seq, rest), F32)], axis=-1)
    a = jnp.concatenate([-sin, jnp.zeros((seq, half + rest), F32)], axis=-1)
    b = jnp.concatenate([jnp.zeros((seq, half), F32), sin, jnp.zeros((seq, rest), F32)], axis=-1)
    scale = MOBA_HEAD_DIM ** -0.5 * jnp.log2(jnp.float32(jnp.e))
    return tuple(jnp.stack([t * scale, t]) for t in (c, a, b))


def _moba_proj_kernel(xn_ref, w_ref, c_ref, a_ref, b_ref, o_ref, wb_ref, *, n_rot_tiles):
    j = pl.program_id(0)

    @pl.when(pl.program_id(1) == 0)
    def _():
        wb_ref[...] = w_ref[...].astype(BF16)

    @pl.when(j < n_rot_tiles)
    def _():
        y = _dot(xn_ref[...], wb_ref[...])
        half = ROPE_DIM // 2
        c, a, b = c_ref[...], a_ref[...], b_ref[...]
        for h in range(y.shape[1] // MOBA_HEAD_DIM):
            sl = slice(h * MOBA_HEAD_DIM, (h + 1) * MOBA_HEAD_DIM)
            yh = y[:, sl]
            up = pltpu.roll(yh, MOBA_HEAD_DIM - half, 1)
            down = pltpu.roll(yh, half, 1)
            o_ref[:, sl] = (yh * c + up * a + down * b).astype(o_ref.dtype)

    @pl.when(j >= n_rot_tiles)
    def _():
        o_ref[...] = _dot(xn_ref[...], wb_ref[...]).astype(o_ref.dtype)


def _moba_proj(xn, w_qkv, layer, seq, *, tm=1024, tn=1024):
    t, d = xn.shape
    n = w_qkv.shape[2]
    tn = min(tn, d)
    c, a, b = _moba_rope_tables(seq)
    pos_tiles = seq // tm
    q_tiles = d // tn
    tab_spec = pl.BlockSpec((None, tm, MOBA_HEAD_DIM),
                            lambda j, i: (jnp.minimum(j // q_tiles, 1), i % pos_tiles, 0))
    return pl.pallas_call(
        functools.partial(_moba_proj_kernel, n_rot_tiles=2 * q_tiles),
        grid=(n // tn, t // tm),
        in_specs=[
            pl.BlockSpec((tm, d), lambda j, i: (i, 0)),
            pl.BlockSpec((None, d, tn), lambda j, i: (layer, 0, j)),
            tab_spec, tab_spec, tab_spec,
        ],
        out_specs=pl.BlockSpec((tm, tn), lambda j, i: (i, j)),
        out_shape=jax.ShapeDtypeStruct((t, n), BF16),
        scratch_shapes=[pltpu.VMEM((d, tn), BF16)],
        compiler_params=_params("arbitrary", "arbitrary"),
        name="moba_proj",
    )(xn, w_qkv, c, a, b)


def _split_bf16(x):
    hi = x.astype(BF16)
    lo = (x - hi.astype(F32)).astype(BF16)
    return hi, lo


def _moba_attn_kernel(q_ref, k_ref, v_ref, o_ref):
    seq = q_ref.shape[0]
    blk = MOBA_BLOCK
    n_blk = seq // blk
    heads = range(q_ref.shape[1] // MOBA_HEAD_DIM)
    cols = [slice(h * MOBA_HEAD_DIM, (h + 1) * MOBA_HEAD_DIM) for h in heads]

    k16 = [k_ref[:, c] for c in cols]
    ones = jnp.ones((BF16_SUBLANES, seq), BF16)
    vt_ext = [jnp.concatenate([v_ref[:, c].astype(F32).T.astype(BF16), ones], axis=0) for c in cols]
    km16 = []
    for k in k16:
        k_mean = jnp.mean(k.astype(F32).reshape(n_blk, blk, MOBA_HEAD_DIM), axis=1)
        km16.append(jnp.concatenate(_split_bf16(k_mean), axis=0))

    blk_row = lax.broadcasted_iota(jnp.int32, (n_blk, blk), 0)
    key_in_blk = lax.broadcasted_iota(jnp.int32, (blk, blk), 0)
    qry_in_blk = lax.broadcasted_iota(jnp.int32, (blk, blk), 1)
    causal = key_in_blk <= qry_in_blk

    for i in range(n_blk):
        rows = slice(i * blk, (i + 1) * blk)
        n_keys = (i + 1) * blk
        q16 = [q_ref[rows, c] for c in cols]
        st = [_dot_nt(k[:n_keys], q) for k, q in zip(k16, q16)]

        gated = i > MOBA_TOPK
        if gated:
            g2 = [_dot_nt(km, q) for km, q in zip(km16, q16)]
            gate = [g[:n_blk] + g[n_blk:] for g in g2]
            past = blk_row < i

        m = [jnp.full((1, blk), jnp.finfo(F32).min, F32) for _ in heads]
        acc = [None for _ in heads]
        for j in range(i + 1):
            for h in heads:
                s = st[h][j * blk:(j + 1) * blk]
                if j == i:
                    s = jnp.where(causal, s, -jnp.inf)
                col_max = jnp.max(s, axis=0, keepdims=True)
                if gated and j < i:
                    gj = gate[h][j:j + 1, :]
                    beats = ((gate[h] > gj) | ((gate[h] == gj) & (blk_row < j))) & past
                    keep = jnp.sum(beats.astype(F32), axis=0, keepdims=True) < MOBA_TOPK
                    m_new = jnp.maximum(m[h], jnp.where(keep, col_max, -jnp.inf))
                    shift = jnp.where(keep, m_new, jnp.inf)
                else:
                    m_new = jnp.maximum(m[h], col_max)
                    shift = m_new
                p = jnp.exp2(s - shift).astype(BF16)
                acc_j = _dot(vt_ext[h][:, j * blk:(j + 1) * blk], p)
                acc[h] = acc_j if j == 0 else jnp.exp2(m[h] - m_new) * acc[h] + acc_j
                m[h] = m_new
        for h in heads:
            out = acc[h][:MOBA_HEAD_DIM] / acc[h][MOBA_HEAD_DIM:MOBA_HEAD_DIM + 1]
            o_ref[rows, cols[h]] = out.T.astype(o_ref.dtype)


def _moba_attn(qkv, batch, seq, d, *, heads_per_step=4):
    groups = d // (MOBA_HEAD_DIM * heads_per_step)
    blk = (seq, MOBA_HEAD_DIM * heads_per_step)
    return pl.pallas_call(
        _moba_attn_kernel,
        grid=(batch, groups),
        in_specs=[
            pl.BlockSpec(blk, lambda b, h: (b, h)),
            pl.BlockSpec(blk, lambda b, h: (b, groups + h)),
            pl.BlockSpec(blk, lambda b, h: (b, 2 * groups + h)),
        ],
        out_specs=pl.BlockSpec(blk, lambda b, h: (b, h)),
        out_shape=jax.ShapeDtypeStruct((batch * seq, d), BF16),
        compiler_params=_params("parallel", "parallel"),
        name="moba_attn",
    )(qkv, qkv, qkv)


def _ret_proj_kernel(xn_ref, w_ref, cos_ref, sin_ref, o_ref, wb_ref, *, n_q_tiles):
    j = pl.program_id(0)
    half = RET_HEAD_DIM // 2

    @pl.when(pl.program_id(1) == 0)
    def _():
        wb_ref[...] = w_ref[...].astype(BF16)

    def rotate(scale):
        y = _dot(xn_ref[...], wb_ref[...])
        cos, sin = cos_ref[...], sin_ref[...]
        for h in range(y.shape[1] // RET_HEAD_DIM):
            lo = h * RET_HEAD_DIM
            x1, x2 = y[:, lo:lo + half], y[:, lo + half:lo + RET_HEAD_DIM]
            o_ref[:, lo:lo + half] = ((x1 * cos - x2 * sin) * scale).astype(o_ref.dtype)
            o_ref[:, lo + half:lo + RET_HEAD_DIM] = ((x1 * sin + x2 * cos) * scale).astype(o_ref.dtype)

    pl.when(j < n_q_tiles)(lambda: rotate(1.0))
    pl.when((j >= n_q_tiles) & (j < 2 * n_q_tiles))(lambda: rotate(RET_HEAD_DIM ** -0.5))

    @pl.when(j >= 2 * n_q_tiles)
    def _():
        o_ref[...] = _dot(xn_ref[...], wb_ref[...]).astype(o_ref.dtype)


def _ret_proj(xn, w_in, layer, seq, *, tm=1024, tn=1024):
    t, d = xn.shape
    n = w_in.shape[2]
    tn = min(tn, d)
    half = RET_HEAD_DIM // 2
    inv_freq = jnp.power(jnp.float32(RET_ROT_BASE), -jnp.linspace(0.0, 1.0, half, dtype=F32))
    ang = jnp.arange(seq).astype(F32)[:, None] * inv_freq[None, :]
    pos_tiles = seq // tm
    tab_spec = pl.BlockSpec((tm, half), lambda j, i: (i % pos_tiles, 0))
    return pl.pallas_call(
        functools.partial(_ret_proj_kernel, n_q_tiles=d // tn),
        grid=(n // tn, t // tm),
        in_specs=[
            pl.BlockSpec((tm, d), lambda j, i: (i, 0)),
            pl.BlockSpec((None, d, tn), lambda j, i: (layer, 0, j)),
            tab_spec, tab_spec,
        ],
        out_specs=pl.BlockSpec((tm, tn), lambda j, i: (i, j)),
        out_shape=jax.ShapeDtypeStruct((t, n), BF16),
        scratch_shapes=[pltpu.VMEM((d, tn), BF16)],
        compiler_params=_params("arbitrary", "arbitrary"),
        name="ret_proj",
    )(xn, w_in, jnp.cos(ang), jnp.sin(ang))


def _ret_kernel(q_ref, k_ref, v_ref, g_ref, gn_ref, dmask_ref, qdec_ref, kdec_ref, cdec_ref, o_ref):
    seq = q_ref.shape[0]
    c = RET_CHUNK
    n_chunks = seq // c
    dmask = dmask_ref[...]
    qdec = qdec_ref[...]
    kdec = kdec_ref[...]
    cdec = cdec_ref[...]
    gn = gn_ref[...]

    state = None
    for n in range(n_chunks):
        rows = slice(n * c, (n + 1) * c)
        qc, kc, vc = q_ref[rows, :], k_ref[rows, :], v_ref[rows, :]
        s = _dot_nt(qc, kc) * dmask
        out = _dot(s.astype(BF16), vc)
        if state is not None:
            out = out + _dot(qc, state.astype(BF16)) * qdec
        if n + 1 < n_chunks:
            kt = (kc.astype(F32) * kdec).T.astype(BF16)
            upd = _dot(kt, vc)
            state = upd if state is None else state * cdec + upd
        out = out * lax.rsqrt(jnp.mean(out * out, axis=-1, keepdims=True) + RMS_EPS) * gn
        o_ref[rows, :] = (_silu(g_ref[rows, :].astype(F32)) * out).astype(o_ref.dtype)


def _retention(proj, gn_gain, layer, batch, seq, d):
    heads = d // RET_HEAD_DIM
    c = RET_CHUNK
    log_gamma = jnp.log1p(-jnp.power(2.0, -5.0 - jnp.arange(heads, dtype=F32)))
    n = jnp.arange(c, dtype=F32)
    diff = n[:, None] - n[None, :]
    dmask = jnp.exp(jnp.where((diff >= 0)[None], diff[None] * log_gamma[:, None, None], -jnp.inf))
    qdec = jnp.exp((n[None, :] + 1.0) * log_gamma[:, None])[:, :, None]
    kdec = jnp.exp((c - 1.0 - n[None, :]) * log_gamma[:, None])[:, :, None]
    cdec = jnp.broadcast_to(jnp.exp(c * log_gamma)[:, None, None], (heads, 1, RET_HEAD_DIM))
    blk = (seq, RET_HEAD_DIM)
    return pl.pallas_call(
        _ret_kernel,
        grid=(batch, heads),
        in_specs=[
            pl.BlockSpec(blk, lambda b, h: (b, h)),
            pl.BlockSpec(blk, lambda b, h: (b, heads + h)),
            pl.BlockSpec(blk, lambda b, h: (b, 2 * heads + h)),
            pl.BlockSpec(blk, lambda b, h: (b, 3 * heads + h)),
            pl.BlockSpec((None, 1, RET_HEAD_DIM), lambda b, h: (layer, 0, h)),
            pl.BlockSpec((None, c, c), lambda b, h: (h, 0, 0)),
            pl.BlockSpec((None, c, 1), lambda b, h: (h, 0, 0)),
            pl.BlockSpec((None, c, 1), lambda b, h: (h, 0, 0)),
            pl.BlockSpec((None, 1, RET_HEAD_DIM), lambda b, h: (h, 0, 0)),
        ],
        out_specs=pl.BlockSpec(blk, lambda b, h: (b, h)),
        out_shape=jax.ShapeDtypeStruct((batch * seq, d), BF16),
        compiler_params=_params("parallel", "parallel"),
        name="retention",
    )(proj, proj, proj, proj, gn_gain.reshape(gn_gain.shape[0], 1, d), dmask, qdec, kdec, cdec)


def kernel(x, norm_gain, ffn_w_gate_up, ffn_w_down, moba_w_qkv, moba_w_o, ret_w_in, ret_w_o, ret_gn_gain, final_norm):
    batch, seq, d = x.shape
    depth = norm_gain.shape[0]
    h = x.reshape(batch * seq, d)
    for i in range(depth):
        g = norm_gain[i]
        h, hn = _ffn(h, g[0], ffn_w_gate_up, ffn_w_down, i, 0, g[1], epilogue="norm_out")
        if i % 2 == 0:
            qkv = _moba_proj(hn, moba_w_qkv, i // 2, seq)
            h = _out_proj(_moba_attn(qkv, batch, seq, d), moba_w_o, i // 2, h)
        else:
            proj = _ret_proj(hn, ret_w_in, i // 2, seq)
            h = _out_proj(_retention(proj, ret_gn_gain, i // 2, batch, seq, d), ret_w_o, i // 2, h)
        h = _ffn(h, g[2], ffn_w_gate_up, ffn_w_down, i, 1, final_norm,
                 epilogue="final_norm" if i == depth - 1 else "none")[0]
    return h.reshape(batch, seq, d)
```

```python
import functools

import jax
import jax.numpy as jnp
from jax import lax
from jax.experimental import pallas as pl
from jax.experimental.pallas import tpu as pltpu

FFN_RES = 0.5
RMS_EPS = 1e-6

MOBA_HEAD_DIM = 128
MOBA_BLOCK = 256
MOBA_TOPK = 3
ROPE_THETA = 500000.0
ROPE_DIM = MOBA_HEAD_DIM // 4

RET_HEAD_DIM = 256
RET_CHUNK = 256
RET_ROT_BASE = 10000.0

V7X_VMEM_BYTES = 64 * 1024 * 1024
BF16_SUBLANES = 16
VMEM_LIMIT_BYTES = V7X_VMEM_BYTES - 8 * 1024 * 1024

F32 = jnp.float32
BF16 = jnp.bfloat16


def _params(*semantics):
    return pltpu.CompilerParams(dimension_semantics=semantics, vmem_limit_bytes=VMEM_LIMIT_BYTES)


def _rms_norm(x, gain):
    return x * lax.rsqrt(jnp.mean(x * x, axis=-1, keepdims=True) + RMS_EPS) * gain


def _dot(a, b):
    return jnp.dot(a, b, preferred_element_type=F32)


def _dot_nt(a, b):
    return lax.dot_general(a, b, (((1,), (1,)), ((), ())), preferred_element_type=F32)


def _silu(g):
    return g * jax.nn.sigmoid(g)


def _ffn_step(j, last_j, x_ref, gain_ref, egain_ref, o_ref, xn_out_ref, xn_ref, weights, epilogue):
    def step(first, last):
        if first:
            xn = _rms_norm(x_ref[...], gain_ref[...]).astype(BF16)
            xn_ref[...] = xn
        else:
            xn = xn_ref[...]
        tiles = weights()
        hidden = []
        for wgu, wd in tiles:
            tf = wd.shape[0]
            gu = _dot(xn, wgu)
            hidden.append((_silu(gu[:, :tf]) * gu[:, tf:]).astype(BF16))
        acc = _dot(jnp.concatenate(hidden, axis=1), jnp.concatenate([wd for _, wd in tiles], axis=0))
        if not first:
            acc = o_ref[...] + acc
        if last:
            y = x_ref[...] + FFN_RES * acc
            if epilogue == "final_norm":
                y = _rms_norm(y, egain_ref[...])
            elif epilogue == "norm_out":
                xn_out_ref[...] = _rms_norm(y, egain_ref[...]).astype(BF16)
            o_ref[...] = y
        else:
            o_ref[...] = acc

    pl.when(j == 0)(lambda: step(True, False))
    pl.when((j > 0) & (j < last_j))(lambda: step(False, False))
    pl.when(j == last_j)(lambda: step(False, True))


def _ffn_head_kernel(x_ref, gain_ref, wg_ref, wu_ref, wd_ref, egain_ref, *rest, epilogue):
    if epilogue == "norm_out":
        _, o_ref, xn_out_ref, wgu16_ref, wd16_ref, xn_ref = rest
    else:
        o_ref, wgu16_ref, wd16_ref, xn_ref = rest
        xn_out_ref = None
    tf = wd_ref.shape[0]
    j = pl.program_id(0)
    tile = pl.ds(pl.multiple_of((j % (wg_ref.shape[1] // tf)) * tf, tf), tf)

    def weights():
        wgu16_ref[:, :tf] = wg_ref[:, tile].astype(BF16)
        wgu16_ref[:, tf:] = wu_ref[:, tile].astype(BF16)
        wd16_ref[...] = wd_ref[...].astype(BF16)
        return [(wgu16_ref[...], wd16_ref[...])]

    _ffn_step(j, pl.num_programs(0) - 1, x_ref, gain_ref, egain_ref, o_ref, xn_out_ref, xn_ref, weights, epilogue)


def _ffn_tail_kernel(x_ref, gain_ref, wgu16_ref, wd16_ref, egain_ref, *rest, epilogue):
    if epilogue == "norm_out":
        _, o_ref, xn_out_ref = rest
        xn_ref = xn_out_ref
    else:
        o_ref, xn_ref = rest
        xn_out_ref = None
    _ffn_step(pl.program_id(1), pl.num_programs(1) - 1, x_ref, gain_ref, egain_ref, o_ref, xn_out_ref, xn_ref,
              lambda: [(wgu16_ref[s], wd16_ref[s]) for s in range(wd16_ref.shape[0])], epilogue)


def _ffn(x, gain, w_gate_up, w_down, layer, which, egain, *, epilogue, tm=1024, tf=256, tail_tiles=2):
    t, d = x.shape
    f = w_down.shape[2]
    nf = f // tf
    assert nf % tail_tiles == 0 and nf // tail_tiles >= 2
    norm_out = epilogue == "norm_out"
    once = pl.Buffered(1)
    any_spec = pl.BlockSpec(memory_space=pl.ANY)
    act_shapes = [jax.ShapeDtypeStruct((t, d), F32)] + ([jax.ShapeDtypeStruct((t, d), BF16)] if norm_out else [])
    extra_in = [jnp.zeros((t, d), BF16)] if norm_out else []

    first_rows = pl.BlockSpec((tm, d), lambda j: (0, 0), pipeline_mode=once)
    head = pl.pallas_call(
        functools.partial(_ffn_head_kernel, epilogue=epilogue),
        grid=(nf,),
        in_specs=[
            first_rows,
            pl.BlockSpec((1, d), lambda j: (0, 0)),
            pl.BlockSpec((None, None, d, tail_tiles * tf), lambda j: (layer, which, 0, j // tail_tiles)),
            pl.BlockSpec((None, None, d, tail_tiles * tf), lambda j: (layer, which, 0, (j + nf) // tail_tiles)),
            pl.BlockSpec((None, None, tf, d), lambda j: (layer, which, j, 0)),
            pl.BlockSpec((1, d), lambda j: (0, 0)),
        ] + [any_spec] * norm_out,
        out_specs=[first_rows] * len(act_shapes) + [
            pl.BlockSpec((None, d, 2 * tf), lambda j: (j, 0, 0)),
            pl.BlockSpec((None, tf, d), lambda j: (j, 0, 0)),
        ],
        out_shape=act_shapes + [jax.ShapeDtypeStruct((nf, d, 2 * tf), BF16),
                                jax.ShapeDtypeStruct((nf, tf, d), BF16)],
        input_output_aliases={0: 0, 6: 1} if norm_out else {0: 0},
        scratch_shapes=[pltpu.VMEM((tm, d), BF16)],
        compiler_params=_params("arbitrary"),
        name="ffn_head",
    )(x, gain.reshape(1, d), w_gate_up, w_gate_up, w_down, egain.reshape(1, d), *extra_in)
    *acts, wgu16, wd16 = head

    def rest_rows(**kw):
        return pl.BlockSpec((tm, d), lambda i, j: (i + 1, 0), **kw)

    act = tm * d
    fixed_bytes = (act * 2 + 2 * act * 4
                   + 2 * tail_tiles * 3 * d * tf * 2
                   + tm * tf * (2 * 4 + 2))
    x_mode = {} if fixed_bytes + 2 * act * 4 <= VMEM_LIMIT_BYTES else {"pipeline_mode": once}

    return pl.pallas_call(
        functools.partial(_ffn_tail_kernel, epilogue=epilogue),
        grid=(t // tm - 1, nf // tail_tiles),
        in_specs=[
            rest_rows(**x_mode),
            pl.BlockSpec((1, d), lambda i, j: (0, 0)),
            pl.BlockSpec((tail_tiles, d, 2 * tf), lambda i, j: (j, 0, 0)),
            pl.BlockSpec((tail_tiles, tf, d), lambda i, j: (j, 0, 0)),
            pl.BlockSpec((1, d), lambda i, j: (0, 0)),
        ] + [any_spec] * norm_out,
        out_specs=[rest_rows()] + [rest_rows(pipeline_mode=once)] * norm_out,
        out_shape=act_shapes,
        input_output_aliases={0: 0, 5: 1} if norm_out else {0: 0},
        scratch_shapes=[] if norm_out else [pltpu.VMEM((tm, d), BF16)],
        compiler_params=_params("parallel", "arbitrary"),
        name="ffn_tail",
    )(acts[0], gain.reshape(1, d), wgu16, wd16, egain.reshape(1, d), *acts[1:])


def _out_proj_kernel(a_ref, w_ref, x_ref, o_ref, wb_ref):
    @pl.when(pl.program_id(0) == 0)
    def _():
        wb_ref[...] = w_ref[...].astype(BF16)

    o_ref[...] = x_ref[...] + _dot(a_ref[...], wb_ref[...])


def _out_proj(a, w, layer, x, *, tm=512):
    t, k = a.shape
    n = w.shape[2]
    return pl.pallas_call(
        _out_proj_kernel,
        grid=(t // tm,),
        in_specs=[
            pl.BlockSpec((tm, k), lambda i: (i, 0)),
            pl.BlockSpec((None, k, n), lambda i: (layer, 0, 0), pipeline_mode=pl.Buffered(1)),
            pl.BlockSpec((tm, n), lambda i: (i, 0)),
        ],
        out_specs=pl.BlockSpec((tm, n), lambda i: (i, 0)),
        out_shape=jax.ShapeDtypeStruct((t, n), F32),
        scratch_shapes=[pltpu.VMEM((k, n), BF16)],
        compiler_params=_params("arbitrary"),
        name="out_proj",
    )(a, w, x)


def _moba_rope_tables(seq):
    half = ROPE_DIM // 2
    inv_freq = jnp.power(jnp.float32(ROPE_THETA), -jnp.arange(half, dtype=F32) / half)
    ang = jnp.arange(seq).astype(F32)[:, None] * inv_freq[None, :]
    cos, sin = jnp.cos(ang), jnp.sin(ang)
    rest = MOBA_HEAD_DIM - ROPE_DIM
    c = jnp.concatenate([cos, cos, jnp.ones((seq, rest), F32)], axis=-1)
    a = jnp.concatenate([-sin, jnp.zeros((seq, half + rest), F32)], axis=-1)
    b = jnp.concatenate([jnp.zeros((seq, half), F32), sin, jnp.zeros((seq, rest), F32)], axis=-1)
    scale = MOBA_HEAD_DIM ** -0.5 * jnp.log2(jnp.float32(jnp.e))
    return tuple(jnp.stack([t * scale, t]) for t in (c, a, b))


def _moba_proj_kernel(xn_ref, w_ref, c_ref, a_ref, b_ref, o_ref, wb_ref, *, n_rot_tiles):
    j = pl.program_id(0)

    @pl.when(pl.program_id(1) == 0)
    def _():
        wb_ref[...] = w_ref[...].astype(BF16)

    @pl.when(j < n_rot_tiles)
    def _():
        y = _dot(xn_ref[...], wb_ref[...])
        half = ROPE_DIM // 2
        c, a, b = c_ref[...], a_ref[...], b_ref[...]
        for h in range(y.shape[1] // MOBA_HEAD_DIM):
            sl = slice(h * MOBA_HEAD_DIM, (h + 1) * MOBA_HEAD_DIM)
            yh = y[:, sl]
            up = pltpu.roll(yh, MOBA_HEAD_DIM - half, 1)
            down = pltpu.roll(yh, half, 1)
            o_ref[:, sl] = (yh * c + up * a + down * b).astype(o_ref.dtype)

    @pl.when(j >= n_rot_tiles)
    def _():
        o_ref[...] = _dot(xn_ref[...], wb_ref[...]).astype(o_ref.dtype)


def _moba_proj(xn, w_qkv, layer, seq, *, tm=1024, tn=1024):
    t, d = xn.shape
    n = w_qkv.shape[2]
    tn = min(tn, d)
    c, a, b = _moba_rope_tables(seq)
    pos_tiles = seq // tm
    q_tiles = d // tn
    tab_spec = pl.BlockSpec((None, tm, MOBA_HEAD_DIM),
                            lambda j, i: (jnp.minimum(j // q_tiles, 1), i % pos_tiles, 0))
    return pl.pallas_call(
        functools.partial(_moba_proj_kernel, n_rot_tiles=2 * q_tiles),
        grid=(n // tn, t // tm),
        in_specs=[
            pl.BlockSpec((tm, d), lambda j, i: (i, 0)),
            pl.BlockSpec((None, d, tn), lambda j, i: (layer, 0, j)),
            tab_spec, tab_spec, tab_spec,
        ],
        out_specs=pl.BlockSpec((tm, tn), lambda j, i: (i, j)),
        out_shape=jax.ShapeDtypeStruct((t, n), BF16),
        scratch_shapes=[pltpu.VMEM((d, tn), BF16)],
        compiler_params=_params("arbitrary", "arbitrary"),
        name="moba_proj",
    )(xn, w_qkv, c, a, b)


def _split_bf16(x):
    hi = x.astype(BF16)
    lo = (x - hi.astype(F32)).astype(BF16)
    return hi, lo


def _moba_attn_kernel(q_ref, k_ref, v_ref, o_ref):
    seq = q_ref.shape[0]
    blk = MOBA_BLOCK
    n_blk = seq // blk
    heads = range(q_ref.shape[1] // MOBA_HEAD_DIM)
    cols = [slice(h * MOBA_HEAD_DIM, (h + 1) * MOBA_HEAD_DIM) for h in heads]

    k16 = [k_ref[:, c] for c in cols]
    ones = jnp.ones((BF16_SUBLANES, seq), BF16)
    vt_ext = [jnp.concatenate([v_ref[:, c].astype(F32).T.astype(BF16), ones], axis=0) for c in cols]
    km16 = []
    for k in k16:
        k_mean = jnp.mean(k.astype(F32).reshape(n_blk, blk, MOBA_HEAD_DIM), axis=1)
        km16.append(jnp.concatenate(_split_bf16(k_mean), axis=0))

    blk_row = lax.broadcasted_iota(jnp.int32, (n_blk, blk), 0)
    key_in_blk = lax.broadcasted_iota(jnp.int32, (blk, blk), 0)
    qry_in_blk = lax.broadcasted_iota(jnp.int32, (blk, blk), 1)
    causal = key_in_blk <= qry_in_blk

    for i in range(n_blk):
        rows = slice(i * blk, (i + 1) * blk)
        n_keys = (i + 1) * blk
        q16 = [q_ref[rows, c] for c in cols]
        st = [_dot_nt(k[:n_keys], q) for k, q in zip(k16, q16)]

        gated = i > MOBA_TOPK
        if gated:
            g2 = [_dot_nt(km, q) for km, q in zip(km16, q16)]
            gate = [g[:n_blk] + g[n_blk:] for g in g2]
            past = blk_row < i

        m = [jnp.full((1, blk), jnp.finfo(F32).min, F32) for _ in heads]
        acc = [None for _ in heads]
        for j in range(i + 1):
            for h in heads:
                s = st[h][j * blk:(j + 1) * blk]
                if j == i:
                    s = jnp.where(causal, s, -jnp.inf)
                col_max = jnp.max(s, axis=0, keepdims=True)
                if gated and j < i:
                    gj = gate[h][j:j + 1, :]
                    beats = ((gate[h] > gj) | ((gate[h] == gj) & (blk_row < j))) & past
                    keep = jnp.sum(beats.astype(F32), axis=0, keepdims=True) < MOBA_TOPK
                    m_new = jnp.maximum(m[h], jnp.where(keep, col_max, -jnp.inf))
                    shift = jnp.where(keep, m_new, jnp.inf)
                else:
                    m_new = jnp.maximum(m[h], col_max)
                    shift = m_new
                p = jnp.exp2(s - shift).astype(BF16)
                acc_j = _dot(vt_ext[h][:, j * blk:(j + 1) * blk], p)
                acc[h] = acc_j if j == 0 else jnp.exp2(m[h] - m_new) * acc[h] + acc_j
                m[h] = m_new
        for h in heads:
            out = acc[h][:MOBA_HEAD_DIM] / acc[h][MOBA_HEAD_DIM:MOBA_HEAD_DIM + 1]
            o_ref[rows, cols[h]] = out.T.astype(o_ref.dtype)


def _moba_attn(qkv, batch, seq, d, *, heads_per_step=4):
    groups = d // (MOBA_HEAD_DIM * heads_per_step)
    blk = (seq, MOBA_HEAD_DIM * heads_per_step)
    return pl.pallas_call(
        _moba_attn_kernel,
        grid=(batch, groups),
        in_specs=[
            pl.BlockSpec(blk, lambda b, h: (b, h)),
            pl.BlockSpec(blk, lambda b, h: (b, groups + h)),
            pl.BlockSpec(blk, lambda b, h: (b, 2 * groups + h)),
        ],
        out_specs=pl.BlockSpec(blk, lambda b, h: (b, h)),
        out_shape=jax.ShapeDtypeStruct((batch * seq, d), BF16),
        compiler_params=_params("parallel", "parallel"),
        name="moba_attn",
    )(qkv, qkv, qkv)


def _ret_proj_kernel(xn_ref, w_ref, cos_ref, sin_ref, o_ref, wb_ref, *, n_q_tiles):
    j = pl.program_id(0)
    half = RET_HEAD_DIM // 2

    @pl.when(pl.program_id(1) == 0)
    def _():
        wb_ref[...] = w_ref[...].astype(BF16)

    def rotate(scale):
        y = _dot(xn_ref[...], wb_ref[...])
        cos, sin = cos_ref[...], sin_ref[...]
        for h in range(y.shape[1] // RET_HEAD_DIM):
            lo = h * RET_HEAD_DIM
            x1, x2 = y[:, lo:lo + half], y[:, lo + half:lo + RET_HEAD_DIM]
            o_ref[:, lo:lo + half] = ((x1 * cos - x2 * sin) * scale).astype(o_ref.dtype)
            o_ref[:, lo + half:lo + RET_HEAD_DIM] = ((x1 * sin + x2 * cos) * scale).astype(o_ref.dtype)

    pl.when(j < n_q_tiles)(lambda: rotate(1.0))
    pl.when((j >= n_q_tiles) & (j < 2 * n_q_tiles))(lambda: rotate(RET_HEAD_DIM ** -0.5))

    @pl.when(j >= 2 * n_q_tiles)
    def _():
        o_ref[...] = _dot(xn_ref[...], wb_ref[...]).astype(o_ref.dtype)


def _ret_proj(xn, w_in, layer, seq, *, tm=1024, tn=1024):
    t, d = xn.shape
    n = w_in.shape[2]
    tn = min(tn, d)
    half = RET_HEAD_DIM // 2
    inv_freq = jnp.power(jnp.float32(RET_ROT_BASE), -jnp.linspace(0.0, 1.0, half, dtype=F32))
    ang = jnp.arange(seq).astype(F32)[:, None] * inv_freq[None, :]
    pos_tiles = seq // tm
    tab_spec = pl.BlockSpec((tm, half), lambda j, i: (i % pos_tiles, 0))
    return pl.pallas_call(
        functools.partial(_ret_proj_kernel, n_q_tiles=d // tn),
        grid=(n // tn, t // tm),
        in_specs=[
            pl.BlockSpec((tm, d), lambda j, i: (i, 0)),
            pl.BlockSpec((None, d, tn), lambda j, i: (layer, 0, j)),
            tab_spec, tab_spec,
        ],
        out_specs=pl.BlockSpec((tm, tn), lambda j, i: (i, j)),
        out_shape=jax.ShapeDtypeStruct((t, n), BF16),
        scratch_shapes=[pltpu.VMEM((d, tn), BF16)],
        compiler_params=_params("arbitrary", "arbitrary"),
        name="ret_proj",
    )(xn, w_in, jnp.cos(ang), jnp.sin(ang))


def _ret_kernel(q_ref, k_ref, v_ref, g_ref, gn_ref, dmask_ref, qdec_ref, kdec_ref, cdec_ref, o_ref):
    seq = q_ref.shape[0]
    c = RET_CHUNK
    n_chunks = seq // c
    heads = range(q_ref.shape[1] // RET_HEAD_DIM)
    cols = [slice(h * RET_HEAD_DIM, (h + 1) * RET_HEAD_DIM) for h in heads]

    state = [None for _ in heads]
    for n in range(n_chunks):
        rows = slice(n * c, (n + 1) * c)
        for h in heads:
            qc, kc, vc = q_ref[rows, cols[h]], k_ref[rows, cols[h]], v_ref[rows, cols[h]]
            s = _dot_nt(qc, kc) * dmask_ref[h]
            out = _dot(s.astype(BF16), vc)
            if state[h] is not None:
                out = out + _dot(qc, state[h].astype(BF16)) * qdec_ref[h]
            if n + 1 < n_chunks:
                kt = (kc.astype(F32) * kdec_ref[h]).T.astype(BF16)
                upd = _dot(kt, vc)
                state[h] = upd if state[h] is None else state[h] * cdec_ref[h] + upd
            out = out * lax.rsqrt(jnp.mean(out * out, axis=-1, keepdims=True) + RMS_EPS) * gn_ref[:, cols[h]]
            o_ref[rows, cols[h]] = (_silu(g_ref[rows, cols[h]].astype(F32)) * out).astype(o_ref.dtype)


def _retention(proj, gn_gain, layer, batch, seq, d, *, heads_per_step=1):
    heads = d // RET_HEAD_DIM
    hps = min(heads_per_step, heads)
    groups = heads // hps
    c = RET_CHUNK
    log_gamma = jnp.log1p(-jnp.power(2.0, -5.0 - jnp.arange(heads, dtype=F32)))
    n = jnp.arange(c, dtype=F32)
    diff = n[:, None] - n[None, :]
    dmask = jnp.exp(jnp.where((diff >= 0)[None], diff[None] * log_gamma[:, None, None], -jnp.inf))
    qdec = jnp.exp((n[None, :] + 1.0) * log_gamma[:, None])[:, :, None]
    kdec = jnp.exp((c - 1.0 - n[None, :]) * log_gamma[:, None])[:, :, None]
    cdec = jnp.broadcast_to(jnp.exp(c * log_gamma)[:, None, None], (heads, 1, RET_HEAD_DIM))
    blk = (seq, hps * RET_HEAD_DIM)
    return pl.pallas_call(
        _ret_kernel,
        grid=(batch, groups),
        in_specs=[
            pl.BlockSpec(blk, lambda b, g: (b, g)),
            pl.BlockSpec(blk, lambda b, g: (b, groups + g)),
            pl.BlockSpec(blk, lambda b, g: (b, 2 * groups + g)),
            pl.BlockSpec(blk, lambda b, g: (b, 3 * groups + g)),
            pl.BlockSpec((None, 1, hps * RET_HEAD_DIM), lambda b, g: (layer, 0, g)),
            pl.BlockSpec((hps, c, c), lambda b, g: (g, 0, 0)),
            pl.BlockSpec((hps, c, 1), lambda b, g: (g, 0, 0)),
            pl.BlockSpec((hps, c, 1), lambda b, g: (g, 0, 0)),
            pl.BlockSpec((hps, 1, RET_HEAD_DIM), lambda b, g: (g, 0, 0)),
        ],
        out_specs=pl.BlockSpec(blk, lambda b, g: (b, g)),
        out_shape=jax.ShapeDtypeStruct((batch * seq, d), BF16),
        compiler_params=_params("parallel", "parallel"),
        name="retention",
    )(proj, proj, proj, proj, gn_gain.reshape(gn_gain.shape[0], 1, d), dmask, qdec, kdec, cdec)


def kernel(x, norm_gain, ffn_w_gate_up, ffn_w_down, moba_w_qkv, moba_w_o, ret_w_in, ret_w_o, ret_gn_gain, final_norm):
    batch, seq, d = x.shape
    depth = norm_gain.shape[0]
    h = x.reshape(batch * seq, d)
    for i in range(depth):
        g = norm_gain[i]
        h, hn = _ffn(h, g[0], ffn_w_gate_up, ffn_w_down, i, 0, g[1], epilogue="norm_out")
        if i % 2 == 0:
            qkv = _moba_proj(hn, moba_w_qkv, i // 2, seq)
            h = _out_proj(_moba_attn(qkv, batch, seq, d), moba_w_o, i // 2, h)
        else:
            proj = _ret_proj(hn, ret_w_in, i // 2, seq)
            h = _out_proj(_retention(proj, ret_gn_gain, i // 2, batch, seq, d), ret_w_o, i // 2, h)
        h = _ffn(h, g[2], ffn_w_gate_up, ffn_w_down, i, 1, final_norm,
                 epilogue="final_norm" if i == depth - 1 else "none")[0]
    return h.reshape(batch, seq, d)
```

```python
import functools

import jax
import jax.numpy as jnp
from jax import lax
from jax.experimental import pallas as pl
from jax.experimental.pallas import tpu as pltpu

FFN_RES = 0.5
RMS_EPS = 1e-6

MOBA_HEAD_DIM = 128
MOBA_BLOCK = 256
MOBA_TOPK = 3
ROPE_THETA = 500000.0
ROPE_DIM = MOBA_HEAD_DIM // 4

RET_HEAD_DIM = 256
RET_CHUNK = 256
RET_ROT_BASE = 10000.0

V7X_VMEM_BYTES = 64 * 1024 * 1024
BF16_SUBLANES = 16
VMEM_LIMIT_BYTES = V7X_VMEM_BYTES - 8 * 1024 * 1024

F32 = jnp.float32
BF16 = jnp.bfloat16


def _params(*semantics):
    return pltpu.CompilerParams(dimension_semantics=semantics, vmem_limit_bytes=VMEM_LIMIT_BYTES)


def _rms_norm(x, gain):
    return x * lax.rsqrt(jnp.mean(x * x, axis=-1, keepdims=True) + RMS_EPS) * gain


def _dot(a, b):
    return jnp.dot(a, b, preferred_element_type=F32)


def _dot_nt(a, b):
    return lax.dot_general(a, b, (((1,), (1,)), ((), ())), preferred_element_type=F32)


def _silu(g):
    return g * jax.nn.sigmoid(g)


def _ffn_step(j, last_j, x_ref, gain_ref, egain_ref, o_ref, xn_out_ref, xn_ref, weights, epilogue):
    def step(first, last):
        if first:
            xn = _rms_norm(x_ref[...], gain_ref[...]).astype(BF16)
            xn_ref[...] = xn
        else:
            xn = xn_ref[...]
        tiles = weights()
        hidden = []
        for wgu, wd in tiles:
            tf = wd.shape[0]
            gu = _dot(xn, wgu)
            hidden.append((_silu(gu[:, :tf]) * gu[:, tf:]).astype(BF16))
        acc = _dot(jnp.concatenate(hidden, axis=1), jnp.concatenate([wd for _, wd in tiles], axis=0))
        if not first:
            acc = o_ref[...] + acc
        if last:
            y = x_ref[...] + FFN_RES * acc
            if epilogue == "final_norm":
                y = _rms_norm(y, egain_ref[...])
            elif epilogue == "norm_out":
                xn_out_ref[...] = _rms_norm(y, egain_ref[...]).astype(BF16)
            o_ref[...] = y
        else:
            o_ref[...] = acc

    pl.when(j == 0)(lambda: step(True, False))
    pl.when((j > 0) & (j < last_j))(lambda: step(False, False))
    pl.when(j == last_j)(lambda: step(False, True))


def _ffn_head_kernel(x_ref, gain_ref, wg_ref, wu_ref, wd_ref, egain_ref, *rest, epilogue):
    if epilogue == "norm_out":
        _, o_ref, xn_out_ref, wgu16_ref, wd16_ref, xn_ref = rest
    else:
        o_ref, wgu16_ref, wd16_ref, xn_ref = rest
        xn_out_ref = None
    tf = wd_ref.shape[0]

    def weights():
        wgu16_ref[:, :tf] = wg_ref[...].astype(BF16)
        wgu16_ref[:, tf:] = wu_ref[...].astype(BF16)
        wd16_ref[...] = wd_ref[...].astype(BF16)
        return [(wgu16_ref[...], wd16_ref[...])]

    _ffn_step(pl.program_id(0), pl.num_programs(0) - 1, x_ref, gain_ref, egain_ref, o_ref, xn_out_ref, xn_ref,
              weights, epilogue)


def _ffn_tail_kernel(x_ref, gain_ref, wgu16_ref, wd16_ref, egain_ref, *rest, epilogue):
    if epilogue == "norm_out":
        _, o_ref, xn_out_ref = rest
        xn_ref = xn_out_ref
    else:
        o_ref, xn_ref = rest
        xn_out_ref = None
    _ffn_step(pl.program_id(1), pl.num_programs(1) - 1, x_ref, gain_ref, egain_ref, o_ref, xn_out_ref, xn_ref,
              lambda: [(wgu16_ref[s], wd16_ref[s]) for s in range(wd16_ref.shape[0])], epilogue)


def _ffn(x, gain, w_gate_up, w_down, layer, which, egain, *, epilogue, tm=1024, tf=256, tail_tiles=2):
    t, d = x.shape
    f = w_down.shape[2]
    nf = f // tf
    assert nf % tail_tiles == 0 and nf // tail_tiles >= 2
    norm_out = epilogue == "norm_out"
    once = pl.Buffered(1)
    any_spec = pl.BlockSpec(memory_space=pl.ANY)
    act_shapes = [jax.ShapeDtypeStruct((t, d), F32)] + ([jax.ShapeDtypeStruct((t, d), BF16)] if norm_out else [])
    extra_in = [jnp.zeros((t, d), BF16)] if norm_out else []

    first_rows = pl.BlockSpec((tm, d), lambda j: (0, 0), pipeline_mode=once)
    head = pl.pallas_call(
        functools.partial(_ffn_head_kernel, epilogue=epilogue),
        grid=(nf,),
        in_specs=[
            first_rows,
            pl.BlockSpec((1, d), lambda j: (0, 0)),
            pl.BlockSpec((None, None, d, tf), lambda j: (layer, which, 0, j)),
            pl.BlockSpec((None, None, d, tf), lambda j: (layer, which, 0, j + nf)),
            pl.BlockSpec((None, None, tf, d), lambda j: (layer, which, j, 0)),
            pl.BlockSpec((1, d), lambda j: (0, 0)),
        ] + [any_spec] * norm_out,
        out_specs=[first_rows] * len(act_shapes) + [
            pl.BlockSpec((None, d, 2 * tf), lambda j: (j, 0, 0)),
            pl.BlockSpec((None, tf, d), lambda j: (j, 0, 0)),
        ],
        out_shape=act_shapes + [jax.ShapeDtypeStruct((nf, d, 2 * tf), BF16),
                                jax.ShapeDtypeStruct((nf, tf, d), BF16)],
        input_output_aliases={0: 0, 6: 1} if norm_out else {0: 0},
        scratch_shapes=[pltpu.VMEM((tm, d), BF16)],
        compiler_params=_params("arbitrary"),
        name="ffn_head",
    )(x, gain.reshape(1, d), w_gate_up, w_gate_up, w_down, egain.reshape(1, d), *extra_in)
    *acts, wgu16, wd16 = head

    def rest_rows(**kw):
        return pl.BlockSpec((tm, d), lambda i, j: (i + 1, 0), **kw)

    act = tm * d
    fixed_bytes = (act * 2 + 2 * act * 4
                   + 2 * tail_tiles * 3 * d * tf * 2
                   + tm * tf * (2 * 4 + 2))
    x_mode = {} if fixed_bytes + 2 * act * 4 <= VMEM_LIMIT_BYTES else {"pipeline_mode": once}

    return pl.pallas_call(
        functools.partial(_ffn_tail_kernel, epilogue=epilogue),
        grid=(t // tm - 1, nf // tail_tiles),
        in_specs=[
            rest_rows(**x_mode),
            pl.BlockSpec((1, d), lambda i, j: (0, 0)),
            pl.BlockSpec((tail_tiles, d, 2 * tf), lambda i, j: (j, 0, 0)),
            pl.BlockSpec((tail_tiles, tf, d), lambda i, j: (j, 0, 0)),
            pl.BlockSpec((1, d), lambda i, j: (0, 0)),
        ] + [any_spec] * norm_out,
        out_specs=[rest_rows()] + [rest_rows(pipeline_mode=once)] * norm_out,
        out_shape=act_shapes,
        input_output_aliases={0: 0, 5: 1} if norm_out else {0: 0},
        scratch_shapes=[] if norm_out else [pltpu.VMEM((tm, d), BF16)],
        compiler_params=_params("parallel", "arbitrary"),
        name="ffn_tail",
    )(acts[0], gain.reshape(1, d), wgu16, wd16, egain.reshape(1, d), *acts[1:])


def _out_proj_kernel(a_ref, w_ref, x_ref, o_ref, wb_ref):
    @pl.when(pl.program_id(0) == 0)
    def _():
        wb_ref[...] = w_ref[...].astype(BF16)

    o_ref[...] = x_ref[...] + _dot(a_ref[...], wb_ref[...])


def _out_proj(a, w, layer, x, *, tm=512):
    t, k = a.shape
    n = w.shape[2]
    return pl.pallas_call(
        _out_proj_kernel,
        grid=(t // tm,),
        in_specs=[
            pl.BlockSpec((tm, k), lambda i: (i, 0)),
            pl.BlockSpec((None, k, n), lambda i: (layer, 0, 0), pipeline_mode=pl.Buffered(1)),
            pl.BlockSpec((tm, n), lambda i: (i, 0)),
        ],
        out_specs=pl.BlockSpec((tm, n), lambda i: (i, 0)),
        out_shape=jax.ShapeDtypeStruct((t, n), F32),
        scratch_shapes=[pltpu.VMEM((k, n), BF16)],
        compiler_params=_params("arbitrary"),
        name="out_proj",
    )(a, w, x)


def _moba_rope_tables(seq):
    half = ROPE_DIM // 2
    inv_freq = jnp.power(jnp.float32(ROPE_THETA), -jnp.arange(half, dtype=F32) / half)
    ang = jnp.arange(seq).astype(F32)[:, None] * inv_freq[None, :]
    cos, sin = jnp.cos(ang), jnp.sin(ang)
    rest = MOBA_HEAD_DIM - ROPE_DIM
    c = jnp.concatenate([cos, cos, jnp.ones((seq, rest), F32)], axis=-1)
    a = jnp.concatenate([-sin, jnp.zeros((seq, half + rest), F32)], axis=-1)
    b = jnp.concatenate([jnp.zeros((seq, half), F32), sin, jnp.zeros((seq, rest), F32)], axis=-1)
    scale = MOBA_HEAD_DIM ** -0.5 * jnp.log2(jnp.float32(jnp.e))
    return tuple(jnp.stack([t * scale, t]) for t in (c, a, b))


def _moba_proj_kernel(xn_ref, w_ref, c_ref, a_ref, b_ref, o_ref, wb_ref, *, n_rot_tiles):
    j = pl.program_id(0)

    @pl.when(pl.program_id(1) == 0)
    def _():
        wb_ref[...] = w_ref[...].astype(BF16)

    @pl.when(j < n_rot_tiles)
    def _():
        y = _dot(xn_ref[...], wb_ref[...])
        half = ROPE_DIM // 2
        c, a, b = c_ref[...], a_ref[...], b_ref[...]
        for h in range(y.shape[1] // MOBA_HEAD_DIM):
            sl = slice(h * MOBA_HEAD_DIM, (h + 1) * MOBA_HEAD_DIM)
            yh = y[:, sl]
            up = pltpu.roll(yh, MOBA_HEAD_DIM - half, 1)
            down = pltpu.roll(yh, half, 1)
            o_ref[:, sl] = (yh * c + up * a + down * b).astype(o_ref.dtype)

    @pl.when(j >= n_rot_tiles)
    def _():
        o_ref[...] = _dot(xn_ref[...], wb_ref[...]).astype(o_ref.dtype)


def _moba_proj(xn, w_qkv, layer, seq, *, tm=1024, tn=1024):
    t, d = xn.shape
    n = w_qkv.shape[2]
    tn = min(tn, d)
    c, a, b = _moba_rope_tables(seq)
    pos_tiles = seq // tm
    q_tiles = d // tn
    tab_spec = pl.BlockSpec((None, tm, MOBA_HEAD_DIM),
                            lambda j, i: (jnp.minimum(j // q_tiles, 1), i % pos_tiles, 0))
    return pl.pallas_call(
        functools.partial(_moba_proj_kernel, n_rot_tiles=2 * q_tiles),
        grid=(n // tn, t // tm),
        in_specs=[
            pl.BlockSpec((tm, d), lambda j, i: (i, 0)),
            pl.BlockSpec((None, d, tn), lambda j, i: (layer, 0, j)),
            tab_spec, tab_spec, tab_spec,
        ],
        out_specs=pl.BlockSpec((tm, tn), lambda j, i: (i, j)),
        out_shape=jax.ShapeDtypeStruct((t, n), BF16),
        scratch_shapes=[pltpu.VMEM((d, tn), BF16)],
        compiler_params=_params("arbitrary", "arbitrary"),
        name="moba_proj",
    )(xn, w_qkv, c, a, b)


def _split_bf16(x):
    hi = x.astype(BF16)
    lo = (x - hi.astype(F32)).astype(BF16)
    return hi, lo


def _moba_attn_kernel(q_ref, k_ref, v_ref, o_ref):
    seq = q_ref.shape[0]
    blk = MOBA_BLOCK
    n_blk = seq // blk
    heads = range(q_ref.shape[1] // MOBA_HEAD_DIM)
    cols = [slice(h * MOBA_HEAD_DIM, (h + 1) * MOBA_HEAD_DIM) for h in heads]

    k16 = [k_ref[:, c] for c in cols]
    ones = jnp.ones((BF16_SUBLANES, seq), BF16)
    vt_ext = [jnp.concatenate([v_ref[:, c].astype(F32).T.astype(BF16), ones], axis=0) for c in cols]
    km16 = []
    for k in k16:
        k_mean = jnp.mean(k.astype(F32).reshape(n_blk, blk, MOBA_HEAD_DIM), axis=1)
        km16.append(jnp.concatenate(_split_bf16(k_mean), axis=0))

    blk_row = lax.broadcasted_iota(jnp.int32, (n_blk, blk), 0)
    key_in_blk = lax.broadcasted_iota(jnp.int32, (blk, blk), 0)
    qry_in_blk = lax.broadcasted_iota(jnp.int32, (blk, blk), 1)
    causal = key_in_blk <= qry_in_blk

    for i in range(n_blk):
        rows = slice(i * blk, (i + 1) * blk)
        n_keys = (i + 1) * blk
        q16 = [q_ref[rows, c] for c in cols]
        st = [_dot_nt(k[:n_keys], q) for k, q in zip(k16, q16)]

        gated = i > MOBA_TOPK
        if gated:
            g2 = [_dot_nt(km, q) for km, q in zip(km16, q16)]
            gate = [g[:n_blk] + g[n_blk:] for g in g2]
            past = blk_row < i

        m = [jnp.full((1, blk), jnp.finfo(F32).min, F32) for _ in heads]
        acc = [None for _ in heads]
        for j in range(i + 1):
            for h in heads:
                s = st[h][j * blk:(j + 1) * blk]
                if j == i:
                    s = jnp.where(causal, s, -jnp.inf)
                col_max = jnp.max(s, axis=0, keepdims=True)
                if gated and j < i:
                    gj = gate[h][j:j + 1, :]
                    beats = ((gate[h] > gj) | ((gate[h] == gj) & (blk_row < j))) & past
                    keep = jnp.sum(beats.astype(F32), axis=0, keepdims=True) < MOBA_TOPK
                    m_new = jnp.maximum(m[h], jnp.where(keep, col_max, -jnp.inf))
                    shift = jnp.where(keep, m_new, jnp.inf)
                else:
                    m_new = jnp.maximum(m[h], col_max)
                    shift = m_new
                p = jnp.exp2(s - shift).astype(BF16)
                acc_j = _dot(vt_ext[h][:, j * blk:(j + 1) * blk], p)
                acc[h] = acc_j if j == 0 else jnp.exp2(m[h] - m_new) * acc[h] + acc_j
                m[h] = m_new
        for h in heads:
            out = acc[h][:MOBA_HEAD_DIM] / acc[h][MOBA_HEAD_DIM:MOBA_HEAD_DIM + 1]
            o_ref[rows, cols[h]] = out.T.astype(o_ref.dtype)


def _moba_attn(qkv, batch, seq, d, *, heads_per_step=4):
    groups = d // (MOBA_HEAD_DIM * heads_per_step)
    blk = (seq, MOBA_HEAD_DIM * heads_per_step)
    return pl.pallas_call(
        _moba_attn_kernel,
        grid=(batch, groups),
        in_specs=[
            pl.BlockSpec(blk, lambda b, h: (b, h)),
            pl.BlockSpec(blk, lambda b, h: (b, groups + h)),
            pl.BlockSpec(blk, lambda b, h: (b, 2 * groups + h)),
        ],
        out_specs=pl.BlockSpec(blk, lambda b, h: (b, h)),
        out_shape=jax.ShapeDtypeStruct((batch * seq, d), BF16),
        compiler_params=_params("parallel", "parallel"),
        name="moba_attn",
    )(qkv, qkv, qkv)


def _ret_proj_kernel(xn_ref, w_ref, cos_ref, sin_ref, o_ref, wb_ref, *, n_q_tiles):
    j = pl.program_id(0)
    half = RET_HEAD_DIM // 2

    @pl.when(pl.program_id(1) == 0)
    def _():
        wb_ref[...] = w_ref[...].astype(BF16)

    def rotate(scale):
        y = _dot(xn_ref[...], wb_ref[...])
        cos, sin = cos_ref[...], sin_ref[...]
        for h in range(y.shape[1] // RET_HEAD_DIM):
            lo = h * RET_HEAD_DIM
            x1, x2 = y[:, lo:lo + half], y[:, lo + half:lo + RET_HEAD_DIM]
            o_ref[:, lo:lo + half] = ((x1 * cos - x2 * sin) * scale).astype(o_ref.dtype)
            o_ref[:, lo + half:lo + RET_HEAD_DIM] = ((x1 * sin + x2 * cos) * scale).astype(o_ref.dtype)

    pl.when(j < n_q_tiles)(lambda: rotate(1.0))
    pl.when((j >= n_q_tiles) & (j < 2 * n_q_tiles))(lambda: rotate(RET_HEAD_DIM ** -0.5))

    @pl.when(j >= 2 * n_q_tiles)
    def _():
        o_ref[...] = _dot(xn_ref[...], wb_ref[...]).astype(o_ref.dtype)


def _ret_proj(xn, w_in, layer, seq, *, tm=1024, tn=1024):
    t, d = xn.shape
    n = w_in.shape[2]
    tn = min(tn, d)
    half = RET_HEAD_DIM // 2
    inv_freq = jnp.power(jnp.float32(RET_ROT_BASE), -jnp.linspace(0.0, 1.0, half, dtype=F32))
    ang = jnp.arange(seq).astype(F32)[:, None] * inv_freq[None, :]
    pos_tiles = seq // tm
    tab_spec = pl.BlockSpec((tm, half), lambda j, i: (i % pos_tiles, 0))
    return pl.pallas_call(
        functools.partial(_ret_proj_kernel, n_q_tiles=d // tn),
        grid=(n // tn, t // tm),
        in_specs=[
            pl.BlockSpec((tm, d), lambda j, i: (i, 0)),
            pl.BlockSpec((None, d, tn), lambda j, i: (layer, 0, j)),
            tab_spec, tab_spec,
        ],
        out_specs=pl.BlockSpec((tm, tn), lambda j, i: (i, j)),
        out_shape=jax.ShapeDtypeStruct((t, n), BF16),
        scratch_shapes=[pltpu.VMEM((d, tn), BF16)],
        compiler_params=_params("arbitrary", "arbitrary"),
        name="ret_proj",
    )(xn, w_in, jnp.cos(ang), jnp.sin(ang))


def _ret_kernel(q_ref, k_ref, v_ref, g_ref, gn_ref, dmask_ref, qdec_ref, kdec_ref, cdec_ref, wo_ref, x_ref,
                o_ref, state_ref, wb_ref):
    n = pl.program_id(1)

    @pl.when((pl.program_id(0) == 0) & (n == 0))
    def _():
        wb_ref[...] = wo_ref[...].astype(BF16)

    @pl.when(n == 0)
    def _():
        state_ref[...] = jnp.zeros_like(state_ref)

    heads = range(state_ref.shape[0])
    cols = [slice(h * RET_HEAD_DIM, (h + 1) * RET_HEAD_DIM) for h in heads]
    qc = [q_ref[:, c] for c in cols]
    kc = [k_ref[:, c] for c in cols]
    vc = [v_ref[:, c] for c in cols]
    s = [(_dot_nt(qc[h], kc[h]) * dmask_ref[h]).astype(BF16) for h in heads]
    cross = [_dot(qc[h], state_ref[h].astype(BF16)) * qdec_ref[h] for h in heads]
    out = [_dot(s[h], vc[h]) + cross[h] for h in heads]
    for h in heads:
        kt = (kc[h].astype(F32) * kdec_ref[h]).T.astype(BF16)
        state_ref[h] = state_ref[h] * cdec_ref[h] + _dot(kt, vc[h])
    for h in heads:
        o = out[h]
        o = o * lax.rsqrt(jnp.mean(o * o, axis=-1, keepdims=True) + RMS_EPS) * gn_ref[:, cols[h]]
        y = (_silu(g_ref[:, cols[h]].astype(F32)) * o).astype(BF16)
        o_ref[...] = (x_ref[...] if h == 0 else o_ref[...]) + _dot(y, wb_ref[cols[h], :])


def _retention(proj, gn_gain, w_o, layer, x, batch, seq):
    t, d = x.shape
    heads = d // RET_HEAD_DIM
    c = RET_CHUNK
    n_chunks = seq // c
    log_gamma = jnp.log1p(-jnp.power(2.0, -5.0 - jnp.arange(heads, dtype=F32)))
    n = jnp.arange(c, dtype=F32)
    diff = n[:, None] - n[None, :]
    dmask = jnp.exp(jnp.where((diff >= 0)[None], diff[None] * log_gamma[:, None, None], -jnp.inf))
    qdec = jnp.exp((n[None, :] + 1.0) * log_gamma[:, None])[:, :, None]
    kdec = jnp.exp((c - 1.0 - n[None, :]) * log_gamma[:, None])[:, :, None]
    cdec = jnp.broadcast_to(jnp.exp(c * log_gamma)[:, None, None], (heads, 1, RET_HEAD_DIM))

    def rows(col):
        return pl.BlockSpec((c, d), lambda b, n: (b * n_chunks + n, col))

    def whole(a):
        return pl.BlockSpec(a.shape, lambda b, n: (0,) * a.ndim)

    return pl.pallas_call(
        _ret_kernel,
        grid=(batch, n_chunks),
        in_specs=[
            rows(0), rows(1), rows(2), rows(3),
            pl.BlockSpec((None, 1, d), lambda b, n: (layer, 0, 0)),
            whole(dmask), whole(qdec), whole(kdec), whole(cdec),
            pl.BlockSpec((None, d, d), lambda b, n: (layer, 0, 0), pipeline_mode=pl.Buffered(1)),
            rows(0),
        ],
        out_specs=rows(0),
        out_shape=jax.ShapeDtypeStruct((t, d), F32),
        scratch_shapes=[pltpu.VMEM((heads, RET_HEAD_DIM, RET_HEAD_DIM), F32), pltpu.VMEM((d, d), BF16)],
        compiler_params=_params("arbitrary", "arbitrary"),
        name="retention",
    )(proj, proj, proj, proj, gn_gain.reshape(gn_gain.shape[0], 1, d), dmask, qdec, kdec, cdec, w_o, x)


def kernel(x, norm_gain, ffn_w_gate_up, ffn_w_down, moba_w_qkv, moba_w_o, ret_w_in, ret_w_o, ret_gn_gain, final_norm):
    batch, seq, d = x.shape
    depth = norm_gain.shape[0]
    h = x.reshape(batch * seq, d)
    for i in range(depth):
        g = norm_gain[i]
        h, hn = _ffn(h, g[0], ffn_w_gate_up, ffn_w_down, i, 0, g[1], epilogue="norm_out")
        if i % 2 == 0:
            qkv = _moba_proj(hn, moba_w_qkv, i // 2, seq)
            h = _out_proj(_moba_attn(qkv, batch, seq, d), moba_w_o, i // 2, h)
        else:
            proj = _ret_proj(hn, ret_w_in, i // 2, seq)
            h = _retention(proj, ret_gn_gain, ret_w_o, i // 2, h, batch, seq)
        h = _ffn(h, g[2], ffn_w_gate_up, ffn_w_down, i, 1, final_norm,
                 epilogue="final_norm" if i == depth - 1 else "none")[0]
    return h.reshape(batch, seq, d)
```

```python
import functools

import jax
import jax.numpy as jnp
from jax import lax
from jax.experimental import pallas as pl
from jax.experimental.pallas import tpu as pltpu

FFN_RES = 0.5
RMS_EPS = 1e-6

MOBA_HEAD_DIM = 128
MOBA_BLOCK = 256
MOBA_TOPK = 3
ROPE_THETA = 500000.0
ROPE_DIM = MOBA_HEAD_DIM // 4

RET_HEAD_DIM = 256
RET_CHUNK = 256
RET_ROT_BASE = 10000.0

V7X_VMEM_BYTES = 64 * 1024 * 1024
BF16_SUBLANES = 16
VMEM_LIMIT_BYTES = V7X_VMEM_BYTES - 8 * 1024 * 1024

F32 = jnp.float32
BF16 = jnp.bfloat16


def _params(*semantics):
    return pltpu.CompilerParams(dimension_semantics=semantics, vmem_limit_bytes=VMEM_LIMIT_BYTES)


def _rms_norm(x, gain):
    return x * lax.rsqrt(jnp.mean(x * x, axis=-1, keepdims=True) + RMS_EPS) * gain


def _dot(a, b):
    return jnp.dot(a, b, preferred_element_type=F32)


def _dot_nt(a, b):
    return lax.dot_general(a, b, (((1,), (1,)), ((), ())), preferred_element_type=F32)


def _silu(g):
    return g * jax.nn.sigmoid(g)


def _ffn_step(j, last_j, x_ref, gain_ref, egain_ref, o_ref, xn_out_ref, xn_ref, weights, epilogue):
    def step(first, last):
        if first:
            xn = _rms_norm(x_ref[...], gain_ref[...]).astype(BF16)
            xn_ref[...] = xn
        else:
            xn = xn_ref[...]
        tiles = weights()
        hidden = []
        for wgu, wd in tiles:
            tf = wd.shape[0]
            gu = _dot(xn, wgu)
            hidden.append((_silu(gu[:, :tf]) * gu[:, tf:]).astype(BF16))
        acc = _dot(jnp.concatenate(hidden, axis=1), jnp.concatenate([wd for _, wd in tiles], axis=0))
        if not first:
            acc = o_ref[...] + acc
        if last:
            y = x_ref[...] + FFN_RES * acc
            if epilogue == "final_norm":
                y = _rms_norm(y, egain_ref[...])
            elif epilogue == "norm_out":
                xn_out_ref[...] = _rms_norm(y, egain_ref[...]).astype(BF16)
            o_ref[...] = y
        else:
            o_ref[...] = acc

    pl.when(j == 0)(lambda: step(True, False))
    pl.when((j > 0) & (j < last_j))(lambda: step(False, False))
    pl.when(j == last_j)(lambda: step(False, True))


def _ffn_head_kernel(x_ref, gain_ref, wg_ref, wu_ref, wd_ref, egain_ref, *rest, epilogue):
    if epilogue == "norm_out":
        _, o_ref, xn_out_ref, wgu16_ref, wd16_ref, xn_ref = rest
    else:
        o_ref, wgu16_ref, wd16_ref, xn_ref = rest
        xn_out_ref = None
    tf = wd_ref.shape[0]

    def weights():
        wgu16_ref[:, :tf] = wg_ref[...].astype(BF16)
        wgu16_ref[:, tf:] = wu_ref[...].astype(BF16)
        wd16_ref[...] = wd_ref[...].astype(BF16)
        return [(wgu16_ref[...], wd16_ref[...])]

    _ffn_step(pl.program_id(0), pl.num_programs(0) - 1, x_ref, gain_ref, egain_ref, o_ref, xn_out_ref, xn_ref,
              weights, epilogue)


def _ffn_tail_kernel(x_ref, gain_ref, wgu16_ref, wd16_ref, egain_ref, *rest, epilogue):
    if epilogue == "norm_out":
        _, o_ref, xn_out_ref = rest
        xn_ref = xn_out_ref
    else:
        o_ref, xn_ref = rest
        xn_out_ref = None
    _ffn_step(pl.program_id(1), pl.num_programs(1) - 1, x_ref, gain_ref, egain_ref, o_ref, xn_out_ref, xn_ref,
              lambda: [(wgu16_ref[s], wd16_ref[s]) for s in range(wd16_ref.shape[0])], epilogue)


def _ffn(x, gain, w_gate_up, w_down, layer, which, egain, *, epilogue, tm=1024, tf=256, tail_tiles=2):
    t, d = x.shape
    f = w_down.shape[2]
    nf = f // tf
    assert nf % tail_tiles == 0 and nf // tail_tiles >= 2
    norm_out = epilogue == "norm_out"
    once = pl.Buffered(1)
    any_spec = pl.BlockSpec(memory_space=pl.ANY)
    act_shapes = [jax.ShapeDtypeStruct((t, d), F32)] + ([jax.ShapeDtypeStruct((t, d), BF16)] if norm_out else [])
    extra_in = [jnp.zeros((t, d), BF16)] if norm_out else []

    first_rows = pl.BlockSpec((tm, d), lambda j: (0, 0), pipeline_mode=once)
    head = pl.pallas_call(
        functools.partial(_ffn_head_kernel, epilogue=epilogue),
        grid=(nf,),
        in_specs=[
            first_rows,
            pl.BlockSpec((1, d), lambda j: (0, 0)),
            pl.BlockSpec((None, None, d, tf), lambda j: (layer, which, 0, j)),
            pl.BlockSpec((None, None, d, tf), lambda j: (layer, which, 0, j + nf)),
            pl.BlockSpec((None, None, tf, d), lambda j: (layer, which, j, 0)),
            pl.BlockSpec((1, d), lambda j: (0, 0)),
        ] + [any_spec] * norm_out,
        out_specs=[first_rows] * len(act_shapes) + [
            pl.BlockSpec((None, d, 2 * tf), lambda j: (j, 0, 0)),
            pl.BlockSpec((None, tf, d), lambda j: (j, 0, 0)),
        ],
        out_shape=act_shapes + [jax.ShapeDtypeStruct((nf, d, 2 * tf), BF16),
                                jax.ShapeDtypeStruct((nf, tf, d), BF16)],
        input_output_aliases={0: 0, 6: 1} if norm_out else {0: 0},
        scratch_shapes=[pltpu.VMEM((tm, d), BF16)],
        compiler_params=_params("arbitrary"),
        name="ffn_head",
    )(x, gain.reshape(1, d), w_gate_up, w_gate_up, w_down, egain.reshape(1, d), *extra_in)
    *acts, wgu16, wd16 = head

    def rest_rows(**kw):
        return pl.BlockSpec((tm, d), lambda i, j: (i + 1, 0), **kw)

    act = tm * d
    fixed_bytes = (act * 2 + 2 * act * 4
                   + 2 * tail_tiles * 3 * d * tf * 2
                   + tm * tf * (2 * 4 + 2))
    x_mode = {} if fixed_bytes + 2 * act * 4 <= VMEM_LIMIT_BYTES else {"pipeline_mode": once}

    return pl.pallas_call(
        functools.partial(_ffn_tail_kernel, epilogue=epilogue),
        grid=(t // tm - 1, nf // tail_tiles),
        in_specs=[
            rest_rows(**x_mode),
            pl.BlockSpec((1, d), lambda i, j: (0, 0)),
            pl.BlockSpec((tail_tiles, d, 2 * tf), lambda i, j: (j, 0, 0)),
            pl.BlockSpec((tail_tiles, tf, d), lambda i, j: (j, 0, 0)),
            pl.BlockSpec((1, d), lambda i, j: (0, 0)),
        ] + [any_spec] * norm_out,
        out_specs=[rest_rows()] + [rest_rows(pipeline_mode=once)] * norm_out,
        out_shape=act_shapes,
        input_output_aliases={0: 0, 5: 1} if norm_out else {0: 0},
        scratch_shapes=[] if norm_out else [pltpu.VMEM((tm, d), BF16)],
        compiler_params=_params("parallel", "arbitrary"),
        name="ffn_tail",
    )(acts[0], gain.reshape(1, d), wgu16, wd16, egain.reshape(1, d), *acts[1:])


def _out_proj_kernel(a_ref, w_ref, x_ref, o_ref, wb_ref):
    @pl.when(pl.program_id(0) == 0)
    def _():
        wb_ref[...] = w_ref[...].astype(BF16)

    o_ref[...] = x_ref[...] + _dot(a_ref[...], wb_ref[...])


def _out_proj(a, w, layer, x, *, tm=512):
    t, k = a.shape
    n = w.shape[2]
    return pl.pallas_call(
        _out_proj_kernel,
        grid=(t // tm,),
        in_specs=[
            pl.BlockSpec((tm, k), lambda i: (i, 0)),
            pl.BlockSpec((None, k, n), lambda i: (layer, 0, 0), pipeline_mode=pl.Buffered(1)),
            pl.BlockSpec((tm, n), lambda i: (i, 0)),
        ],
        out_specs=pl.BlockSpec((tm, n), lambda i: (i, 0)),
        out_shape=jax.ShapeDtypeStruct((t, n), F32),
        scratch_shapes=[pltpu.VMEM((k, n), BF16)],
        compiler_params=_params("arbitrary"),
        name="out_proj",
    )(a, w, x)


def _moba_rope_tables(seq):
    half = ROPE_DIM // 2
    inv_freq = jnp.power(jnp.float32(ROPE_THETA), -jnp.arange(half, dtype=F32) / half)
    ang = jnp.arange(seq).astype(F32)[:, None] * inv_freq[None, :]
    cos, sin = jnp.cos(ang), jnp.sin(ang)
    rest = MOBA_HEAD_DIM - ROPE_DIM
    c = jnp.concatenate([cos, cos, jnp.ones((seq, rest), F32)], axis=-1)
    a = jnp.concatenate([-sin, jnp.zeros((seq, half + rest), F32)], axis=-1)
    b = jnp.concatenate([jnp.zeros((seq, half), F32), sin, jnp.zeros((seq, rest), F32)], axis=-1)
    scale = MOBA_HEAD_DIM ** -0.5 * jnp.log2(jnp.float32(jnp.e))
    return tuple(jnp.stack([t * scale, t]) for t in (c, a, b))


def _moba_proj_kernel(xn_ref, w_ref, c_ref, a_ref, b_ref, o_ref, wb_ref, *, n_rot_tiles):
    j = pl.program_id(0)

    @pl.when(pl.program_id(1) == 0)
    def _():
        wb_ref[...] = w_ref[...].astype(BF16)

    @pl.when(j < n_rot_tiles)
    def _():
        y = _dot(xn_ref[...], wb_ref[...])
        half = ROPE_DIM // 2
        c, a, b = c_ref[...], a_ref[...], b_ref[...]
        for h in range(y.shape[1] // MOBA_HEAD_DIM):
            sl = slice(h * MOBA_HEAD_DIM, (h + 1) * MOBA_HEAD_DIM)
            yh = y[:, sl]
            up = pltpu.roll(yh, MOBA_HEAD_DIM - half, 1)
            down = pltpu.roll(yh, half, 1)
            o_ref[:, sl] = (yh * c + up * a + down * b).astype(o_ref.dtype)

    @pl.when(j >= n_rot_tiles)
    def _():
        o_ref[...] = _dot(xn_ref[...], wb_ref[...]).astype(o_ref.dtype)


def _moba_proj(xn, w_qkv, layer, seq, *, tm=1024, tn=1024):
    t, d = xn.shape
    n = w_qkv.shape[2]
    tn = min(tn, d)
    c, a, b = _moba_rope_tables(seq)
    pos_tiles = seq // tm
    q_tiles = d // tn
    tab_spec = pl.BlockSpec((None, tm, MOBA_HEAD_DIM),
                            lambda j, i: (jnp.minimum(j // q_tiles, 1), i % pos_tiles, 0))
    return pl.pallas_call(
        functools.partial(_moba_proj_kernel, n_rot_tiles=2 * q_tiles),
        grid=(n // tn, t // tm),
        in_specs=[
            pl.BlockSpec((tm, d), lambda j, i: (i, 0)),
            pl.BlockSpec((None, d, tn), lambda j, i: (layer, 0, j)),
            tab_spec, tab_spec, tab_spec,
        ],
        out_specs=pl.BlockSpec((tm, tn), lambda j, i: (i, j)),
        out_shape=jax.ShapeDtypeStruct((t, n), BF16),
        scratch_shapes=[pltpu.VMEM((d, tn), BF16)],
        compiler_params=_params("arbitrary", "arbitrary"),
        name="moba_proj",
    )(xn, w_qkv, c, a, b)


def _split_bf16(x):
    hi = x.astype(BF16)
    lo = (x - hi.astype(F32)).astype(BF16)
    return hi, lo


def _moba_attn_kernel(q_ref, k_ref, v_ref, o_ref):
    seq = q_ref.shape[0]
    blk = MOBA_BLOCK
    n_blk = seq // blk
    heads = range(q_ref.shape[1] // MOBA_HEAD_DIM)
    cols = [slice(h * MOBA_HEAD_DIM, (h + 1) * MOBA_HEAD_DIM) for h in heads]

    k16 = [k_ref[:, c] for c in cols]
    ones = jnp.ones((BF16_SUBLANES, seq), BF16)
    vt_ext = [jnp.concatenate([v_ref[:, c].astype(F32).T.astype(BF16), ones], axis=0) for c in cols]
    km16 = []
    for k in k16:
        k_mean = jnp.mean(k.astype(F32).reshape(n_blk, blk, MOBA_HEAD_DIM), axis=1)
        km16.append(jnp.concatenate(_split_bf16(k_mean), axis=0))

    blk_row = lax.broadcasted_iota(jnp.int32, (n_blk, blk), 0)
    key_in_blk = lax.broadcasted_iota(jnp.int32, (blk, blk), 0)
    qry_in_blk = lax.broadcasted_iota(jnp.int32, (blk, blk), 1)
    causal = key_in_blk <= qry_in_blk

    def run(chains):
        st, gate, m, acc = {}, {}, {}, {}
        for ch in chains:
            h, i = ch
            q16 = q_ref[i * blk:(i + 1) * blk, cols[h]]
            st[ch] = _dot_nt(k16[h][:(i + 1) * blk], q16)
            if i > MOBA_TOPK:
                g2 = _dot_nt(km16[h], q16)
                gate[ch] = g2[:n_blk] + g2[n_blk:]
            m[ch] = jnp.full((1, blk), jnp.finfo(F32).min, F32)
            acc[ch] = None
        for j in range(max(i for _, i in chains) + 1):
            live = [ch for ch in chains if j <= ch[1]]
            p, m_new = {}, {}
            for ch in live:
                h, i = ch
                s = st[ch][j * blk:(j + 1) * blk]
                if j == i:
                    s = jnp.where(causal, s, -jnp.inf)
                col_max = jnp.max(s, axis=0, keepdims=True)
                if i > MOBA_TOPK and j < i:
                    gj = gate[ch][j:j + 1, :]
                    beats = ((gate[ch] > gj) | ((gate[ch] == gj) & (blk_row < j))) & (blk_row < i)
                    keep = jnp.sum(beats.astype(F32), axis=0, keepdims=True) < MOBA_TOPK
                    m_new[ch] = jnp.maximum(m[ch], jnp.where(keep, col_max, -jnp.inf))
                    shift = jnp.where(keep, m_new[ch], jnp.inf)
                else:
                    m_new[ch] = jnp.maximum(m[ch], col_max)
                    shift = m_new[ch]
                p[ch] = jnp.exp2(s - shift).astype(BF16)
            acc_j = {ch: _dot(vt_ext[ch[0]][:, j * blk:(j + 1) * blk], p[ch]) for ch in live}
            for ch in live:
                acc[ch] = acc_j[ch] if j == 0 else jnp.exp2(m[ch] - m_new[ch]) * acc[ch] + acc_j[ch]
                m[ch] = m_new[ch]
        for h, i in chains:
            out = acc[h, i][:MOBA_HEAD_DIM] / acc[h, i][MOBA_HEAD_DIM:MOBA_HEAD_DIM + 1]
            o_ref[i * blk:(i + 1) * blk, cols[h]] = out.T.astype(o_ref.dtype)

    for i in range(n_blk // 2):
        run([(h, ib) for ib in (i, n_blk - 1 - i) for h in heads])


def _moba_attn(qkv, batch, seq, d, *, heads_per_step=4):
    groups = d // (MOBA_HEAD_DIM * heads_per_step)
    blk = (seq, MOBA_HEAD_DIM * heads_per_step)
    return pl.pallas_call(
        _moba_attn_kernel,
        grid=(batch, groups),
        in_specs=[
            pl.BlockSpec(blk, lambda b, h: (b, h)),
            pl.BlockSpec(blk, lambda b, h: (b, groups + h)),
            pl.BlockSpec(blk, lambda b, h: (b, 2 * groups + h)),
        ],
        out_specs=pl.BlockSpec(blk, lambda b, h: (b, h)),
        out_shape=jax.ShapeDtypeStruct((batch * seq, d), BF16),
        compiler_params=_params("parallel", "parallel"),
        name="moba_attn",
    )(qkv, qkv, qkv)


def _ret_proj_kernel(xn_ref, w_ref, cos_ref, sin_ref, o_ref, wb_ref, *, n_q_tiles):
    j = pl.program_id(0)
    half = RET_HEAD_DIM // 2

    @pl.when(pl.program_id(1) == 0)
    def _():
        wb_ref[...] = w_ref[...].astype(BF16)

    def rotate(scale):
        y = _dot(xn_ref[...], wb_ref[...])
        cos, sin = cos_ref[...], sin_ref[...]
        for h in range(y.shape[1] // RET_HEAD_DIM):
            lo = h * RET_HEAD_DIM
            x1, x2 = y[:, lo:lo + half], y[:, lo + half:lo + RET_HEAD_DIM]
            o_ref[:, lo:lo + half] = ((x1 * cos - x2 * sin) * scale).astype(o_ref.dtype)
            o_ref[:, lo + half:lo + RET_HEAD_DIM] = ((x1 * sin + x2 * cos) * scale).astype(o_ref.dtype)

    pl.when(j < n_q_tiles)(lambda: rotate(1.0))
    pl.when((j >= n_q_tiles) & (j < 2 * n_q_tiles))(lambda: rotate(RET_HEAD_DIM ** -0.5))

    @pl.when(j >= 2 * n_q_tiles)
    def _():
        o_ref[...] = _dot(xn_ref[...], wb_ref[...]).astype(o_ref.dtype)


def _ret_proj(xn, w_in, layer, seq, *, tm=1024, tn=1024):
    t, d = xn.shape
    n = w_in.shape[2]
    tn = min(tn, d)
    half = RET_HEAD_DIM // 2
    inv_freq = jnp.power(jnp.float32(RET_ROT_BASE), -jnp.linspace(0.0, 1.0, half, dtype=F32))
    ang = jnp.arange(seq).astype(F32)[:, None] * inv_freq[None, :]
    pos_tiles = seq // tm
    tab_spec = pl.BlockSpec((tm, half), lambda j, i: (i % pos_tiles, 0))
    return pl.pallas_call(
        functools.partial(_ret_proj_kernel, n_q_tiles=d // tn),
        grid=(n // tn, t // tm),
        in_specs=[
            pl.BlockSpec((tm, d), lambda j, i: (i, 0)),
            pl.BlockSpec((None, d, tn), lambda j, i: (layer, 0, j)),
            tab_spec, tab_spec,
        ],
        out_specs=pl.BlockSpec((tm, tn), lambda j, i: (i, j)),
        out_shape=jax.ShapeDtypeStruct((t, n), BF16),
        scratch_shapes=[pltpu.VMEM((d, tn), BF16)],
        compiler_params=_params("arbitrary", "arbitrary"),
        name="ret_proj",
    )(xn, w_in, jnp.cos(ang), jnp.sin(ang))


def _ret_kernel(q_ref, k_ref, v_ref, g_ref, gn_ref, dmask_ref, qdec_ref, kdec_ref, cdec_ref, wo_ref, x_ref,
                o_ref, state_ref, wb_ref):
    n = pl.program_id(1)

    @pl.when((pl.program_id(0) == 0) & (n == 0))
    def _():
        wb_ref[...] = wo_ref[...].astype(BF16)

    @pl.when(n == 0)
    def _():
        state_ref[...] = jnp.zeros_like(state_ref)

    heads = range(state_ref.shape[0])
    cols = [slice(h * RET_HEAD_DIM, (h + 1) * RET_HEAD_DIM) for h in heads]
    qc = [q_ref[:, c] for c in cols]
    kc = [k_ref[:, c] for c in cols]
    vc = [v_ref[:, c] for c in cols]
    s = [(_dot_nt(qc[h], kc[h]) * dmask_ref[h]).astype(BF16) for h in heads]
    cross = [_dot(qc[h], state_ref[h].astype(BF16)) * qdec_ref[h] for h in heads]
    out = [_dot(s[h], vc[h]) + cross[h] for h in heads]
    for h in heads:
        kt = (kc[h].astype(F32) * kdec_ref[h]).T.astype(BF16)
        state_ref[h] = state_ref[h] * cdec_ref[h] + _dot(kt, vc[h])
    for h in heads:
        o = out[h]
        o = o * lax.rsqrt(jnp.mean(o * o, axis=-1, keepdims=True) + RMS_EPS) * gn_ref[:, cols[h]]
        y = (_silu(g_ref[:, cols[h]].astype(F32)) * o).astype(BF16)
        o_ref[...] = (x_ref[...] if h == 0 else o_ref[...]) + _dot(y, wb_ref[cols[h], :])


def _retention(proj, gn_gain, w_o, layer, x, batch, seq):
    t, d = x.shape
    heads = d // RET_HEAD_DIM
    c = RET_CHUNK
    n_chunks = seq // c
    log_gamma = jnp.log1p(-jnp.power(2.0, -5.0 - jnp.arange(heads, dtype=F32)))
    n = jnp.arange(c, dtype=F32)
    diff = n[:, None] - n[None, :]
    dmask = jnp.exp(jnp.where((diff >= 0)[None], diff[None] * log_gamma[:, None, None], -jnp.inf))
    qdec = jnp.exp((n[None, :] + 1.0) * log_gamma[:, None])[:, :, None]
    kdec = jnp.exp((c - 1.0 - n[None, :]) * log_gamma[:, None])[:, :, None]
    cdec = jnp.broadcast_to(jnp.exp(c * log_gamma)[:, None, None], (heads, 1, RET_HEAD_DIM))

    def rows(col):
        return pl.BlockSpec((c, d), lambda b, n: (b * n_chunks + n, col))

    def whole(a):
        return pl.BlockSpec(a.shape, lambda b, n: (0,) * a.ndim)

    return pl.pallas_call(
        _ret_kernel,
        grid=(batch, n_chunks),
        in_specs=[
            rows(0), rows(1), rows(2), rows(3),
            pl.BlockSpec((None, 1, d), lambda b, n: (layer, 0, 0)),
            whole(dmask), whole(qdec), whole(kdec), whole(cdec),
            pl.BlockSpec((None, d, d), lambda b, n: (layer, 0, 0), pipeline_mode=pl.Buffered(1)),
            rows(0),
        ],
        out_specs=rows(0),
        out_shape=jax.ShapeDtypeStruct((t, d), F32),
        scratch_shapes=[pltpu.VMEM((heads, RET_HEAD_DIM, RET_HEAD_DIM), F32), pltpu.VMEM((d, d), BF16)],
        compiler_params=_params("arbitrary", "arbitrary"),
        name="retention",
    )(proj, proj, proj, proj, gn_gain.reshape(gn_gain.shape[0], 1, d), dmask, qdec, kdec, cdec, w_o, x)


def kernel(x, norm_gain, ffn_w_gate_up, ffn_w_down, moba_w_qkv, moba_w_o, ret_w_in, ret_w_o, ret_gn_gain, final_norm):
    batch, seq, d = x.shape
    depth = norm_gain.shape[0]
    h = x.reshape(batch * seq, d)
    for i in range(depth):
        g = norm_gain[i]
        h, hn = _ffn(h, g[0], ffn_w_gate_up, ffn_w_down, i, 0, g[1], epilogue="norm_out")
        if i % 2 == 0:
            qkv = _moba_proj(hn, moba_w_qkv, i // 2, seq)
            h = _out_proj(_moba_attn(qkv, batch, seq, d), moba_w_o, i // 2, h)
        else:
            proj = _ret_proj(hn, ret_w_in, i // 2, seq)
            h = _retention(proj, ret_gn_gain, ret_w_o, i // 2, h, batch, seq)
        h = _ffn(h, g[2], ffn_w_gate_up, ffn_w_down, i, 1, final_norm,
                 epilogue="final_norm" if i == depth - 1 else "none")[0]
    return h.reshape(batch, seq, d)
```

```python
import functools

import jax
import jax.numpy as jnp
from jax import lax
from jax.experimental import pallas as pl
from jax.experimental.pallas import tpu as pltpu

FFN_RES = 0.5
RMS_EPS = 1e-6

MOBA_HEAD_DIM = 128
MOBA_BLOCK = 256
MOBA_TOPK = 3
ROPE_THETA = 500000.0
ROPE_DIM = MOBA_HEAD_DIM // 4

RET_HEAD_DIM = 256
RET_CHUNK = 256
RET_ROT_BASE = 10000.0

V7X_VMEM_BYTES = 64 * 1024 * 1024
BF16_SUBLANES = 16
VMEM_LIMIT_BYTES = V7X_VMEM_BYTES - 8 * 1024 * 1024

F32 = jnp.float32
BF16 = jnp.bfloat16


def _params(*semantics):
    return pltpu.CompilerParams(dimension_semantics=semantics, vmem_limit_bytes=VMEM_LIMIT_BYTES)


def _rms_norm(x, gain):
    return x * lax.rsqrt(jnp.mean(x * x, axis=-1, keepdims=True) + RMS_EPS) * gain


def _dot(a, b):
    return jnp.dot(a, b, preferred_element_type=F32)


def _dot_nt(a, b):
    return lax.dot_general(a, b, (((1,), (1,)), ((), ())), preferred_element_type=F32)


def _silu(g):
    return g * jax.nn.sigmoid(g)


def _ffn_step(j, last_j, x_ref, gain_ref, egain_ref, o_ref, xn_out_ref, xn_ref, weights, epilogue):
    def step(first, last):
        if first:
            xn = _rms_norm(x_ref[...], gain_ref[...]).astype(BF16)
            xn_ref[...] = xn
        else:
            xn = xn_ref[...]
        tiles = weights()
        hidden = []
        for wgu, wd in tiles:
            tf = wd.shape[0]
            gu = _dot(xn, wgu)
            hidden.append((_silu(gu[:, :tf]) * gu[:, tf:]).astype(BF16))
        acc = _dot(jnp.concatenate(hidden, axis=1), jnp.concatenate([wd for _, wd in tiles], axis=0))
        if not first:
            acc = o_ref[...] + acc
        if last:
            y = x_ref[...] + FFN_RES * acc
            if epilogue == "final_norm":
                y = _rms_norm(y, egain_ref[...])
            elif epilogue == "norm_out":
                xn_out_ref[...] = _rms_norm(y, egain_ref[...]).astype(BF16)
            o_ref[...] = y
        else:
            o_ref[...] = acc

    pl.when(j == 0)(lambda: step(True, False))
    pl.when((j > 0) & (j < last_j))(lambda: step(False, False))
    pl.when(j == last_j)(lambda: step(False, True))


def _ffn_head_kernel(x_ref, gain_ref, wg_ref, wu_ref, wd_ref, egain_ref, *rest, epilogue, n_unused):
    rest = rest[n_unused:]
    if epilogue == "norm_out":
        o_ref, xn_out_ref, wgu16_ref, wd16_ref, xn_ref = rest
    else:
        o_ref, wgu16_ref, wd16_ref, xn_ref = rest
        xn_out_ref = None
    tf = wd_ref.shape[0]

    def weights():
        wgu16_ref[:, :tf] = wg_ref[...].astype(BF16)
        wgu16_ref[:, tf:] = wu_ref[...].astype(BF16)
        wd16_ref[...] = wd_ref[...].astype(BF16)
        return [(wgu16_ref[...], wd16_ref[...])]

    _ffn_step(pl.program_id(0), pl.num_programs(0) - 1, x_ref, gain_ref, egain_ref, o_ref, xn_out_ref, xn_ref,
              weights, epilogue)


def _ffn_tail_kernel(x_ref, gain_ref, wgu16_ref, wd16_ref, egain_ref, *rest, epilogue, n_unused, skip_first_tile):
    rest = rest[n_unused:]
    if epilogue == "norm_out":
        o_ref, xn_out_ref = rest
        xn_ref = xn_out_ref
    else:
        o_ref, xn_ref = rest
        xn_out_ref = None

    def steps():
        _ffn_step(pl.program_id(1), pl.num_programs(1) - 1, x_ref, gain_ref, egain_ref, o_ref, xn_out_ref, xn_ref,
                  lambda: [(wgu16_ref[s], wd16_ref[s]) for s in range(wd16_ref.shape[0])], epilogue)

    if skip_first_tile:
        i = pl.program_id(0)

        @pl.when((i == 0) & (pl.program_id(1) == 0))
        def _():
            o_ref[...] = jnp.zeros_like(o_ref)
            if xn_out_ref is not None:
                xn_out_ref[...] = jnp.zeros_like(xn_out_ref)

        pl.when(i > 0)(steps)
    else:
        steps()


def _ffn(x, gain, w_gate_up, w_down, layer, which, egain, *, epilogue, in_place=True, tm=1024, tf=256, tail_tiles=2):
    t, d = x.shape
    f = w_down.shape[2]
    nf = f // tf
    assert nf % tail_tiles == 0 and nf // tail_tiles >= 2
    norm_out = epilogue == "norm_out"
    once = pl.Buffered(1)
    any_spec = pl.BlockSpec(memory_space=pl.ANY)
    n_act = 2 if norm_out else 1
    full_shapes = [jax.ShapeDtypeStruct((t, d), F32), jax.ShapeDtypeStruct((t, d), BF16)][:n_act]
    tile_shapes = [jax.ShapeDtypeStruct((tm, d), F32), jax.ShapeDtypeStruct((tm, d), BF16)][:n_act]
    norm_buf = [jnp.zeros((t, d), BF16)] if norm_out and in_place else []

    first_rows = pl.BlockSpec((tm, d), lambda j: (0, 0), pipeline_mode=once)
    head = pl.pallas_call(
        functools.partial(_ffn_head_kernel, epilogue=epilogue, n_unused=len(norm_buf)),
        grid=(nf,),
        in_specs=[
            first_rows,
            pl.BlockSpec((1, d), lambda j: (0, 0)),
            pl.BlockSpec((None, None, d, tf), lambda j: (layer, which, 0, j)),
            pl.BlockSpec((None, None, d, tf), lambda j: (layer, which, 0, j + nf)),
            pl.BlockSpec((None, None, tf, d), lambda j: (layer, which, j, 0)),
            pl.BlockSpec((1, d), lambda j: (0, 0)),
        ] + [any_spec] * len(norm_buf),
        out_specs=[first_rows] * n_act + [
            pl.BlockSpec((None, d, 2 * tf), lambda j: (j, 0, 0)),
            pl.BlockSpec((None, tf, d), lambda j: (j, 0, 0)),
        ],
        out_shape=(full_shapes if in_place else tile_shapes) + [jax.ShapeDtypeStruct((nf, d, 2 * tf), BF16),
                                                                jax.ShapeDtypeStruct((nf, tf, d), BF16)],
        input_output_aliases=({0: 0, 6: 1} if norm_out else {0: 0}) if in_place else {},
        scratch_shapes=[pltpu.VMEM((tm, d), BF16)],
        compiler_params=_params("arbitrary"),
        name="ffn_head",
    )(x, gain.reshape(1, d), w_gate_up, w_gate_up, w_down, egain.reshape(1, d), *norm_buf)
    *acts, wgu16, wd16 = head

    first = 1 if in_place else 0

    def rows(**kw):
        return pl.BlockSpec((tm, d), lambda i, j: (i + first, 0), **kw)

    def weight_tile(i, j):
        return (j if in_place else jnp.where(i == 0, 0, j), 0, 0)

    act = tm * d
    fixed_bytes = (act * 2 + 2 * act * 4
                   + 2 * tail_tiles * 3 * d * tf * 2
                   + tm * tf * (2 * 4 + 2))
    x_mode = {} if fixed_bytes + 2 * act * 4 <= VMEM_LIMIT_BYTES else {"pipeline_mode": once}

    tail = pl.pallas_call(
        functools.partial(_ffn_tail_kernel, epilogue=epilogue, n_unused=len(norm_buf), skip_first_tile=not in_place),
        grid=(t // tm - first, nf // tail_tiles),
        in_specs=[
            rows(**x_mode),
            pl.BlockSpec((1, d), lambda i, j: (0, 0)),
            pl.BlockSpec((tail_tiles, d, 2 * tf), weight_tile),
            pl.BlockSpec((tail_tiles, tf, d), weight_tile),
            pl.BlockSpec((1, d), lambda i, j: (0, 0)),
        ] + [any_spec] * len(norm_buf),
        out_specs=[rows()] + [rows(pipeline_mode=once)] * norm_out,
        out_shape=full_shapes,
        input_output_aliases=({0: 0, 5: 1} if norm_out else {0: 0}) if in_place else {},
        scratch_shapes=[] if norm_out else [pltpu.VMEM((tm, d), BF16)],
        compiler_params=_params("parallel", "arbitrary"),
        name="ffn_tail",
    )(acts[0] if in_place else x, gain.reshape(1, d), wgu16, wd16, egain.reshape(1, d),
      *(acts[1:] if in_place else []))
    if in_place:
        return tail
    return [lax.dynamic_update_slice(full, tile, (0, 0)) for full, tile in zip(tail, acts)]


def _out_proj_kernel(a_ref, w_ref, x_ref, o_ref, wb_ref):
    @pl.when(pl.program_id(0) == 0)
    def _():
        wb_ref[...] = w_ref[...].astype(BF16)

    o_ref[...] = x_ref[...] + _dot(a_ref[...], wb_ref[...])


def _out_proj(a, w, layer, x, *, tm=512):
    t, k = a.shape
    n = w.shape[2]
    return pl.pallas_call(
        _out_proj_kernel,
        grid=(t // tm,),
        in_specs=[
            pl.BlockSpec((tm, k), lambda i: (i, 0)),
            pl.BlockSpec((None, k, n), lambda i: (layer, 0, 0), pipeline_mode=pl.Buffered(1)),
            pl.BlockSpec((tm, n), lambda i: (i, 0)),
        ],
        out_specs=pl.BlockSpec((tm, n), lambda i: (i, 0)),
        out_shape=jax.ShapeDtypeStruct((t, n), F32),
        scratch_shapes=[pltpu.VMEM((k, n), BF16)],
        compiler_params=_params("arbitrary"),
        name="out_proj",
    )(a, w, x)


def _moba_rope_tables(seq):
    half = ROPE_DIM // 2
    inv_freq = jnp.power(jnp.float32(ROPE_THETA), -jnp.arange(half, dtype=F32) / half)
    ang = jnp.arange(seq).astype(F32)[:, None] * inv_freq[None, :]
    cos, sin = jnp.cos(ang), jnp.sin(ang)
    rest = MOBA_HEAD_DIM - ROPE_DIM
    c = jnp.concatenate([cos, cos, jnp.ones((seq, rest), F32)], axis=-1)
    a = jnp.concatenate([-sin, jnp.zeros((seq, half + rest), F32)], axis=-1)
    b = jnp.concatenate([jnp.zeros((seq, half), F32), sin, jnp.zeros((seq, rest), F32)], axis=-1)
    scale = MOBA_HEAD_DIM ** -0.5 * jnp.log2(jnp.float32(jnp.e))
    return tuple(jnp.stack([t * scale, t]) for t in (c, a, b))


def _moba_proj_kernel(xn_ref, w_ref, c_ref, a_ref, b_ref, o_ref, wb_ref, *, n_rot_tiles):
    j = pl.program_id(0)

    @pl.when(pl.program_id(1) == 0)
    def _():
        wb_ref[...] = w_ref[...].astype(BF16)

    @pl.when(j < n_rot_tiles)
    def _():
        y = _dot(xn_ref[...], wb_ref[...])
        half = ROPE_DIM // 2
        c, a, b = c_ref[...], a_ref[...], b_ref[...]
        for h in range(y.shape[1] // MOBA_HEAD_DIM):
            sl = slice(h * MOBA_HEAD_DIM, (h + 1) * MOBA_HEAD_DIM)
            yh = y[:, sl]
            up = pltpu.roll(yh, MOBA_HEAD_DIM - half, 1)
            down = pltpu.roll(yh, half, 1)
            o_ref[:, sl] = (yh * c + up * a + down * b).astype(o_ref.dtype)

    @pl.when(j >= n_rot_tiles)
    def _():
        o_ref[...] = _dot(xn_ref[...], wb_ref[...]).astype(o_ref.dtype)


def _moba_proj(xn, w_qkv, layer, seq, *, tm=1024, tn=1024):
    t, d = xn.shape
    n = w_qkv.shape[2]
    tn = min(tn, d)
    c, a, b = _moba_rope_tables(seq)
    pos_tiles = seq // tm
    q_tiles = d // tn
    tab_spec = pl.BlockSpec((None, tm, MOBA_HEAD_DIM),
                            lambda j, i: (jnp.minimum(j // q_tiles, 1), i % pos_tiles, 0))
    return pl.pallas_call(
        functools.partial(_moba_proj_kernel, n_rot_tiles=2 * q_tiles),
        grid=(n // tn, t // tm),
        in_specs=[
            pl.BlockSpec((tm, d), lambda j, i: (i, 0)),
            pl.BlockSpec((None, d, tn), lambda j, i: (layer, 0, j)),
            tab_spec, tab_spec, tab_spec,
        ],
        out_specs=pl.BlockSpec((tm, tn), lambda j, i: (i, j)),
        out_shape=jax.ShapeDtypeStruct((t, n), BF16),
        scratch_shapes=[pltpu.VMEM((d, tn), BF16)],
        compiler_params=_params("arbitrary", "arbitrary"),
        name="moba_proj",
    )(xn, w_qkv, c, a, b)


def _split_bf16(x):
    hi = x.astype(BF16)
    lo = (x - hi.astype(F32)).astype(BF16)
    return hi, lo


def _moba_attn_kernel(q_ref, k_ref, v_ref, o_ref):
    seq = q_ref.shape[0]
    blk = MOBA_BLOCK
    n_blk = seq // blk
    heads = range(q_ref.shape[1] // MOBA_HEAD_DIM)
    cols = [slice(h * MOBA_HEAD_DIM, (h + 1) * MOBA_HEAD_DIM) for h in heads]

    k16 = [k_ref[:, c] for c in cols]
    ones = jnp.ones((BF16_SUBLANES, seq), BF16)
    vt_ext = [jnp.concatenate([v_ref[:, c].astype(F32).T.astype(BF16), ones], axis=0) for c in cols]
    km16 = []
    for k in k16:
        k_mean = jnp.mean(k.astype(F32).reshape(n_blk, blk, MOBA_HEAD_DIM), axis=1)
        km16.append(jnp.concatenate(_split_bf16(k_mean), axis=0))

    blk_row = lax.broadcasted_iota(jnp.int32, (n_blk, blk), 0)
    key_in_blk = lax.broadcasted_iota(jnp.int32, (blk, blk), 0)
    qry_in_blk = lax.broadcasted_iota(jnp.int32, (blk, blk), 1)
    causal = key_in_blk <= qry_in_blk

    def run(chains):
        st, gate, m, acc = {}, {}, {}, {}
        for ch in chains:
            h, i = ch
            q16 = q_ref[i * blk:(i + 1) * blk, cols[h]]
            st[ch] = _dot_nt(k16[h][:(i + 1) * blk], q16)
            if i > MOBA_TOPK:
                g2 = _dot_nt(km16[h], q16)
                gate[ch] = g2[:n_blk] + g2[n_blk:]
            m[ch] = jnp.full((1, blk), jnp.finfo(F32).min, F32)
            acc[ch] = None
        for j in range(max(i for _, i in chains) + 1):
            live = [ch for ch in chains if j <= ch[1]]
            p, m_new = {}, {}
            for ch in live:
                h, i = ch
                s = st[ch][j * blk:(j + 1) * blk]
                if j == i:
                    s = jnp.where(causal, s, -jnp.inf)
                col_max = jnp.max(s, axis=0, keepdims=True)
                if i > MOBA_TOPK and j < i:
                    gj = gate[ch][j:j + 1, :]
                    beats = ((gate[ch] > gj) | ((gate[ch] == gj) & (blk_row < j))) & (blk_row < i)
                    keep = jnp.sum(beats.astype(F32), axis=0, keepdims=True) < MOBA_TOPK
                    m_new[ch] = jnp.maximum(m[ch], jnp.where(keep, col_max, -jnp.inf))
                    shift = jnp.where(keep, m_new[ch], jnp.inf)
                else:
                    m_new[ch] = jnp.maximum(m[ch], col_max)
                    shift = m_new[ch]
                p[ch] = jnp.exp2(s - shift).astype(BF16)
            acc_j = {ch: _dot(vt_ext[ch[0]][:, j * blk:(j + 1) * blk], p[ch]) for ch in live}
            for ch in live:
                acc[ch] = acc_j[ch] if j == 0 else jnp.exp2(m[ch] - m_new[ch]) * acc[ch] + acc_j[ch]
                m[ch] = m_new[ch]
        for h, i in chains:
            out = acc[h, i][:MOBA_HEAD_DIM] / acc[h, i][MOBA_HEAD_DIM:MOBA_HEAD_DIM + 1]
            o_ref[i * blk:(i + 1) * blk, cols[h]] = out.T.astype(o_ref.dtype)

    for i in range(n_blk // 2):
        run([(h, ib) for h in heads for ib in (n_blk - 1 - i, i)])


def _moba_attn(qkv, batch, seq, d, *, heads_per_step=4):
    groups = d // (MOBA_HEAD_DIM * heads_per_step)
    blk = (seq, MOBA_HEAD_DIM * heads_per_step)
    return pl.pallas_call(
        _moba_attn_kernel,
        grid=(batch, groups),
        in_specs=[
            pl.BlockSpec(blk, lambda b, h: (b, h)),
            pl.BlockSpec(blk, lambda b, h: (b, groups + h)),
            pl.BlockSpec(blk, lambda b, h: (b, 2 * groups + h)),
        ],
        out_specs=pl.BlockSpec(blk, lambda b, h: (b, h)),
        out_shape=jax.ShapeDtypeStruct((batch * seq, d), BF16),
        compiler_params=_params("parallel", "parallel"),
        name="moba_attn",
    )(qkv, qkv, qkv)


def _ret_proj_kernel(xn_ref, w_ref, cos_ref, sin_ref, o_ref, wb_ref, *, n_q_tiles):
    j = pl.program_id(0)
    half = RET_HEAD_DIM // 2

    @pl.when(pl.program_id(1) == 0)
    def _():
        wb_ref[...] = w_ref[...].astype(BF16)

    def rotate(scale):
        y = _dot(xn_ref[...], wb_ref[...])
        cos, sin = cos_ref[...], sin_ref[...]
        for h in range(y.shape[1] // RET_HEAD_DIM):
            lo = h * RET_HEAD_DIM
            x1, x2 = y[:, lo:lo + half], y[:, lo + half:lo + RET_HEAD_DIM]
            o_ref[:, lo:lo + half] = ((x1 * cos - x2 * sin) * scale).astype(o_ref.dtype)
            o_ref[:, lo + half:lo + RET_HEAD_DIM] = ((x1 * sin + x2 * cos) * scale).astype(o_ref.dtype)

    pl.when(j < n_q_tiles)(lambda: rotate(1.0))
    pl.when((j >= n_q_tiles) & (j < 2 * n_q_tiles))(lambda: rotate(RET_HEAD_DIM ** -0.5))

    @pl.when(j >= 2 * n_q_tiles)
    def _():
        o_ref[...] = _dot(xn_ref[...], wb_ref[...]).astype(o_ref.dtype)


def _ret_proj(xn, w_in, layer, seq, *, tm=1024, tn=1024):
    t, d = xn.shape
    n = w_in.shape[2]
    tn = min(tn, d)
    half = RET_HEAD_DIM // 2
    inv_freq = jnp.power(jnp.float32(RET_ROT_BASE), -jnp.linspace(0.0, 1.0, half, dtype=F32))
    ang = jnp.arange(seq).astype(F32)[:, None] * inv_freq[None, :]
    pos_tiles = seq // tm
    tab_spec = pl.BlockSpec((tm, half), lambda j, i: (i % pos_tiles, 0))
    return pl.pallas_call(
        functools.partial(_ret_proj_kernel, n_q_tiles=d // tn),
        grid=(n // tn, t // tm),
        in_specs=[
            pl.BlockSpec((tm, d), lambda j, i: (i, 0)),
            pl.BlockSpec((None, d, tn), lambda j, i: (layer, 0, j)),
            tab_spec, tab_spec,
        ],
        out_specs=pl.BlockSpec((tm, tn), lambda j, i: (i, j)),
        out_shape=jax.ShapeDtypeStruct((t, n), BF16),
        scratch_shapes=[pltpu.VMEM((d, tn), BF16)],
        compiler_params=_params("arbitrary", "arbitrary"),
        name="ret_proj",
    )(xn, w_in, jnp.cos(ang), jnp.sin(ang))


def _ret_kernel(q_ref, k_ref, v_ref, g_ref, gn_ref, dmask_ref, qdec_ref, kdec_ref, cdec_ref, wo_ref, x_ref,
                o_ref, state_ref, wb_ref):
    n = pl.program_id(1)

    @pl.when((pl.program_id(0) == 0) & (n == 0))
    def _():
        wb_ref[...] = wo_ref[...].astype(BF16)

    @pl.when(n == 0)
    def _():
        state_ref[...] = jnp.zeros_like(state_ref)

    heads = range(state_ref.shape[0])
    cols = [slice(h * RET_HEAD_DIM, (h + 1) * RET_HEAD_DIM) for h in heads]
    qc = [q_ref[:, c] for c in cols]
    kc = [k_ref[:, c] for c in cols]
    vc = [v_ref[:, c] for c in cols]
    s = [(_dot_nt(qc[h], kc[h]) * dmask_ref[h]).astype(BF16) for h in heads]
    cross = [_dot(qc[h], state_ref[h].astype(BF16)) * qdec_ref[h] for h in heads]
    out = [_dot(s[h], vc[h]) + cross[h] for h in heads]
    for h in heads:
        kt = (kc[h].astype(F32) * kdec_ref[h]).T.astype(BF16)
        state_ref[h] = state_ref[h] * cdec_ref[h] + _dot(kt, vc[h])
    for h in heads:
        o = out[h]
        o = o * lax.rsqrt(jnp.mean(o * o, axis=-1, keepdims=True) + RMS_EPS) * gn_ref[:, cols[h]]
        y = (_silu(g_ref[:, cols[h]].astype(F32)) * o).astype(BF16)
        o_ref[...] = (x_ref[...] if h == 0 else o_ref[...]) + _dot(y, wb_ref[cols[h], :])


def _retention(proj, gn_gain, w_o, layer, x, batch, seq):
    t, d = x.shape
    heads = d // RET_HEAD_DIM
    c = RET_CHUNK
    n_chunks = seq // c
    log_gamma = jnp.log1p(-jnp.power(2.0, -5.0 - jnp.arange(heads, dtype=F32)))
    n = jnp.arange(c, dtype=F32)
    diff = n[:, None] - n[None, :]
    dmask = jnp.exp(jnp.where((diff >= 0)[None], diff[None] * log_gamma[:, None, None], -jnp.inf))
    qdec = jnp.exp((n[None, :] + 1.0) * log_gamma[:, None])[:, :, None]
    kdec = jnp.exp((c - 1.0 - n[None, :]) * log_gamma[:, None])[:, :, None]
    cdec = jnp.broadcast_to(jnp.exp(c * log_gamma)[:, None, None], (heads, 1, RET_HEAD_DIM))

    def rows(col):
        return pl.BlockSpec((c, d), lambda b, n: (b * n_chunks + n, col))

    def whole(a):
        return pl.BlockSpec(a.shape, lambda b, n: (0,) * a.ndim)

    return pl.pallas_call(
        _ret_kernel,
        grid=(batch, n_chunks),
        in_specs=[
            rows(0), rows(1), rows(2), rows(3),
            pl.BlockSpec((None, 1, d), lambda b, n: (layer, 0, 0)),
            whole(dmask), whole(qdec), whole(kdec), whole(cdec),
            pl.BlockSpec((None, d, d), lambda b, n: (layer, 0, 0), pipeline_mode=pl.Buffered(1)),
            rows(0),
        ],
        out_specs=rows(0),
        out_shape=jax.ShapeDtypeStruct((t, d), F32),
        scratch_shapes=[pltpu.VMEM((heads, RET_HEAD_DIM, RET_HEAD_DIM), F32), pltpu.VMEM((d, d), BF16)],
        compiler_params=_params("arbitrary", "arbitrary"),
        name="retention",
    )(proj, proj, proj, proj, gn_gain.reshape(gn_gain.shape[0], 1, d), dmask, qdec, kdec, cdec, w_o, x)


def kernel(x, norm_gain, ffn_w_gate_up, ffn_w_down, moba_w_qkv, moba_w_o, ret_w_in, ret_w_o, ret_gn_gain, final_norm):
    batch, seq, d = x.shape
    depth = norm_gain.shape[0]
    h = x.reshape(batch * seq, d)
    for i in range(depth):
        g = norm_gain[i]
        h, hn = _ffn(h, g[0], ffn_w_gate_up, ffn_w_down, i, 0, g[1], epilogue="norm_out", in_place=i > 0)
        if i % 2 == 0:
            qkv = _moba_proj(hn, moba_w_qkv, i // 2, seq)
            h = _out_proj(_moba_attn(qkv, batch, seq, d), moba_w_o, i // 2, h)
        else:
            proj = _ret_proj(hn, ret_w_in, i // 2, seq)
            h = _retention(proj, ret_gn_gain, ret_w_o, i // 2, h, batch, seq)
        h = _ffn(h, g[2], ffn_w_gate_up, ffn_w_down, i, 1, final_norm,
                 epilogue="final_norm" if i == depth - 1 else "none")[0]
    return h.reshape(batch, seq, d)
```

```python
import functools

import jax
import jax.numpy as jnp
import numpy as np
from jax import lax
from jax.experimental import pallas as pl
from jax.experimental.pallas import tpu as pltpu

FFN_RES = 0.5
RMS_EPS = 1e-6

MOBA_HEAD_DIM = 128
MOBA_BLOCK = 256
MOBA_TOPK = 3
ROPE_THETA = 500000.0
ROPE_DIM = MOBA_HEAD_DIM // 4

RET_HEAD_DIM = 256
RET_CHUNK = 256
RET_ROT_BASE = 10000.0

V7X_VMEM_BYTES = 64 * 1024 * 1024
BF16_SUBLANES = 16
VMEM_LIMIT_BYTES = V7X_VMEM_BYTES - 8 * 1024 * 1024

F32 = jnp.float32
BF16 = jnp.bfloat16


def _params(*semantics):
    return pltpu.CompilerParams(dimension_semantics=semantics, vmem_limit_bytes=VMEM_LIMIT_BYTES)


def _rms_norm(x, gain):
    return x * lax.rsqrt(jnp.mean(x * x, axis=-1, keepdims=True) + RMS_EPS) * gain


def _dot(a, b):
    return jnp.dot(a, b, preferred_element_type=F32)


def _dot_nt(a, b):
    return lax.dot_general(a, b, (((1,), (1,)), ((), ())), preferred_element_type=F32)


def _silu(g):
    return g * jax.nn.sigmoid(g)


def _ffn_step(j, last_j, x_ref, gain_ref, egain_ref, o_ref, xn_out_ref, xn_ref, weights, epilogue):
    def step(first, last):
        if first:
            xn = _rms_norm(x_ref[...], gain_ref[...]).astype(BF16)
            xn_ref[...] = xn
        else:
            xn = xn_ref[...]
        tiles = weights()
        hidden = []
        for wgu, wd in tiles:
            tf = wd.shape[0]
            gu = _dot(xn, wgu)
            hidden.append((_silu(gu[:, :tf]) * gu[:, tf:]).astype(BF16))
        acc = _dot(jnp.concatenate(hidden, axis=1), jnp.concatenate([wd for _, wd in tiles], axis=0))
        if not first:
            acc = o_ref[...] + acc
        if last:
            y = x_ref[...] + FFN_RES * acc
            if epilogue == "final_norm":
                y = _rms_norm(y, egain_ref[...])
            elif epilogue == "norm_out":
                xn_out_ref[...] = _rms_norm(y, egain_ref[...]).astype(BF16)
            o_ref[...] = y
        else:
            o_ref[...] = acc

    pl.when(j == 0)(lambda: step(True, False))
    pl.when((j > 0) & (j < last_j))(lambda: step(False, False))
    pl.when(j == last_j)(lambda: step(False, True))


def _ffn_head_kernel(x_ref, gain_ref, wg_ref, wu_ref, wd_ref, egain_ref, *rest, epilogue, n_unused):
    rest = rest[n_unused:]
    if epilogue == "norm_out":
        o_ref, xn_out_ref, wgu16_ref, wd16_ref, xn_ref = rest
    else:
        o_ref, wgu16_ref, wd16_ref, xn_ref = rest
        xn_out_ref = None
    tf = wd_ref.shape[0]

    def weights():
        wgu16_ref[:, :tf] = wg_ref[...].astype(BF16)
        wgu16_ref[:, tf:] = wu_ref[...].astype(BF16)
        wd16_ref[...] = wd_ref[...].astype(BF16)
        return [(wgu16_ref[...], wd16_ref[...])]

    _ffn_step(pl.program_id(0), pl.num_programs(0) - 1, x_ref, gain_ref, egain_ref, o_ref, xn_out_ref, xn_ref,
              weights, epilogue)


def _ffn_tail_kernel(x_ref, gain_ref, wgu16_ref, wd16_ref, egain_ref, *rest, epilogue, n_unused, skip_first_tile):
    rest = rest[n_unused:]
    if epilogue == "norm_out":
        o_ref, xn_out_ref = rest
        xn_ref = xn_out_ref
    else:
        o_ref, xn_ref = rest
        xn_out_ref = None

    def steps():
        _ffn_step(pl.program_id(1), pl.num_programs(1) - 1, x_ref, gain_ref, egain_ref, o_ref, xn_out_ref, xn_ref,
                  lambda: [(wgu16_ref[s], wd16_ref[s]) for s in range(wd16_ref.shape[0])], epilogue)

    if skip_first_tile:
        i = pl.program_id(0)

        @pl.when((i == 0) & (pl.program_id(1) == 0))
        def _():
            o_ref[...] = jnp.zeros_like(o_ref)
            if xn_out_ref is not None:
                xn_out_ref[...] = jnp.zeros_like(xn_out_ref)

        pl.when(i > 0)(steps)
    else:
        steps()


def _ffn(x, gain, w_gate_up, w_down, layer, which, egain, *, epilogue, in_place=True, spare=None,
         tm=1024, tf=256, tail_tiles=2):
    t, d = x.shape
    f = w_down.shape[2]
    nf = f // tf
    assert nf % tail_tiles == 0 and nf // tail_tiles >= 2
    norm_out = epilogue == "norm_out"
    once = pl.Buffered(1)
    any_spec = pl.BlockSpec(memory_space=pl.ANY)
    n_act = 2 if norm_out else 1
    full_shapes = [jax.ShapeDtypeStruct((t, d), F32), jax.ShapeDtypeStruct((t, d), BF16)][:n_act]
    tile_shapes = [jax.ShapeDtypeStruct((tm, d), F32), jax.ShapeDtypeStruct((tm, d), BF16)][:n_act]
    norm_buf = [jnp.zeros((t, d), BF16) if spare is None else spare] if norm_out and in_place else []

    first_rows = pl.BlockSpec((tm, d), lambda j: (0, 0), pipeline_mode=once)
    head = pl.pallas_call(
        functools.partial(_ffn_head_kernel, epilogue=epilogue, n_unused=len(norm_buf)),
        grid=(nf,),
        in_specs=[
            first_rows,
            pl.BlockSpec((1, d), lambda j: (0, 0)),
            pl.BlockSpec((None, None, d, tf), lambda j: (layer, which, 0, j)),
            pl.BlockSpec((None, None, d, tf), lambda j: (layer, which, 0, j + nf)),
            pl.BlockSpec((None, None, tf, d), lambda j: (layer, which, j, 0)),
            pl.BlockSpec((1, d), lambda j: (0, 0)),
        ] + [any_spec] * len(norm_buf),
        out_specs=[first_rows] * n_act + [
            pl.BlockSpec((None, d, 2 * tf), lambda j: (j, 0, 0)),
            pl.BlockSpec((None, tf, d), lambda j: (j, 0, 0)),
        ],
        out_shape=(full_shapes if in_place else tile_shapes) + [jax.ShapeDtypeStruct((nf, d, 2 * tf), BF16),
                                                                jax.ShapeDtypeStruct((nf, tf, d), BF16)],
        input_output_aliases=({0: 0, 6: 1} if norm_out else {0: 0}) if in_place else {},
        scratch_shapes=[pltpu.VMEM((tm, d), BF16)],
        compiler_params=_params("arbitrary"),
        name="ffn_head",
    )(x, gain.reshape(1, d), w_gate_up, w_gate_up, w_down, egain.reshape(1, d), *norm_buf)
    *acts, wgu16, wd16 = head

    first = 1 if in_place else 0

    def rows(**kw):
        return pl.BlockSpec((tm, d), lambda i, j: (i + first, 0), **kw)

    def weight_tile(i, j):
        return (j if in_place else jnp.where(i == 0, 0, j), 0, 0)

    act = tm * d
    fixed_bytes = (act * 2 + 2 * act * 4
                   + 2 * tail_tiles * 3 * d * tf * 2
                   + tm * tf * (2 * 4 + 2))
    x_mode = {} if fixed_bytes + 2 * act * 4 <= VMEM_LIMIT_BYTES else {"pipeline_mode": once}

    tail = pl.pallas_call(
        functools.partial(_ffn_tail_kernel, epilogue=epilogue, n_unused=len(norm_buf), skip_first_tile=not in_place),
        grid=(t // tm - first, nf // tail_tiles),
        in_specs=[
            rows(**x_mode),
            pl.BlockSpec((1, d), lambda i, j: (0, 0)),
            pl.BlockSpec((tail_tiles, d, 2 * tf), weight_tile),
            pl.BlockSpec((tail_tiles, tf, d), weight_tile),
            pl.BlockSpec((1, d), lambda i, j: (0, 0)),
        ] + [any_spec] * len(norm_buf),
        out_specs=[rows()] + [rows(pipeline_mode=once)] * norm_out,
        out_shape=full_shapes,
        input_output_aliases=({0: 0, 5: 1} if norm_out else {0: 0}) if in_place else {},
        scratch_shapes=[] if norm_out else [pltpu.VMEM((tm, d), BF16)],
        compiler_params=_params("parallel", "arbitrary"),
        name="ffn_tail",
    )(acts[0] if in_place else x, gain.reshape(1, d), wgu16, wd16, egain.reshape(1, d),
      *(acts[1:] if in_place else []))
    if in_place:
        return tail
    return [lax.dynamic_update_slice(full, tile, (0, 0)) for full, tile in zip(tail, acts)]


def _out_proj_kernel(a_ref, w_ref, x_ref, o_ref, wb_ref):
    @pl.when(pl.program_id(0) == 0)
    def _():
        wb_ref[...] = w_ref[...].astype(BF16)

    o_ref[...] = x_ref[...] + _dot(a_ref[...], wb_ref[...])


def _out_proj(a, w, layer, x, *, tm=512):
    t, k = a.shape
    n = w.shape[2]
    return pl.pallas_call(
        _out_proj_kernel,
        grid=(t // tm,),
        in_specs=[
            pl.BlockSpec((tm, k), lambda i: (i, 0)),
            pl.BlockSpec((None, k, n), lambda i: (layer, 0, 0), pipeline_mode=pl.Buffered(1)),
            pl.BlockSpec((tm, n), lambda i: (i, 0)),
        ],
        out_specs=pl.BlockSpec((tm, n), lambda i: (i, 0)),
        out_shape=jax.ShapeDtypeStruct((t, n), F32),
        scratch_shapes=[pltpu.VMEM((k, n), BF16)],
        compiler_params=_params("arbitrary"),
        name="out_proj",
    )(a, w, x)


def _moba_rope_tables(seq):
    half = ROPE_DIM // 2
    f32 = np.float32
    inv_freq = np.power(f32(ROPE_THETA), -np.arange(half, dtype=f32) / f32(half))
    ang = np.arange(seq, dtype=f32)[:, None] * inv_freq[None, :]
    cos, sin = np.cos(ang), np.sin(ang)
    rest = MOBA_HEAD_DIM - ROPE_DIM
    c = np.concatenate([cos, cos, np.ones((seq, rest), f32)], axis=-1)
    a = np.concatenate([-sin, np.zeros((seq, half + rest), f32)], axis=-1)
    b = np.concatenate([np.zeros((seq, half), f32), sin, np.zeros((seq, rest), f32)], axis=-1)
    scale = f32(MOBA_HEAD_DIM ** -0.5) * np.log2(f32(np.e))
    return tuple(np.stack([t * scale, t]) for t in (c, a, b))


def _moba_proj_kernel(xn_ref, w_ref, c_ref, a_ref, b_ref, o_ref, wb_ref, *, n_rot_tiles):
    j = pl.program_id(0)

    @pl.when(pl.program_id(1) == 0)
    def _():
        wb_ref[...] = w_ref[...].astype(BF16)

    @pl.when(j < n_rot_tiles)
    def _():
        y = _dot(xn_ref[...], wb_ref[...])
        half = ROPE_DIM // 2
        c, a, b = c_ref[...], a_ref[...], b_ref[...]
        for h in range(y.shape[1] // MOBA_HEAD_DIM):
            sl = slice(h * MOBA_HEAD_DIM, (h + 1) * MOBA_HEAD_DIM)
            yh = y[:, sl]
            up = pltpu.roll(yh, MOBA_HEAD_DIM - half, 1)
            down = pltpu.roll(yh, half, 1)
            o_ref[:, sl] = (yh * c + up * a + down * b).astype(o_ref.dtype)

    @pl.when(j >= n_rot_tiles)
    def _():
        o_ref[...] = _dot(xn_ref[...], wb_ref[...]).astype(o_ref.dtype)


def _moba_proj(xn, w_qkv, layer, seq, *, tm=1024, tn=1024):
    t, d = xn.shape
    n = w_qkv.shape[2]
    tn = min(tn, d)
    c, a, b = _moba_rope_tables(seq)
    pos_tiles = seq // tm
    q_tiles = d // tn
    tab_spec = pl.BlockSpec((None, tm, MOBA_HEAD_DIM),
                            lambda j, i: (jnp.minimum(j // q_tiles, 1), i % pos_tiles, 0))
    return pl.pallas_call(
        functools.partial(_moba_proj_kernel, n_rot_tiles=2 * q_tiles),
        grid=(n // tn, t // tm),
        in_specs=[
            pl.BlockSpec((tm, d), lambda j, i: (i, 0)),
            pl.BlockSpec((None, d, tn), lambda j, i: (layer, 0, j)),
            tab_spec, tab_spec, tab_spec,
        ],
        out_specs=pl.BlockSpec((tm, tn), lambda j, i: (i, j)),
        out_shape=jax.ShapeDtypeStruct((t, n), BF16),
        scratch_shapes=[pltpu.VMEM((d, tn), BF16)],
        compiler_params=_params("arbitrary", "arbitrary"),
        name="moba_proj",
    )(xn, w_qkv, c, a, b)


def _split_bf16(x):
    hi = x.astype(BF16)
    lo = (x - hi.astype(F32)).astype(BF16)
    return hi, lo


def _moba_attn_kernel(q_ref, k_ref, v_ref, o_ref):
    seq = q_ref.shape[0]
    blk = MOBA_BLOCK
    n_blk = seq // blk
    heads = range(q_ref.shape[1] // MOBA_HEAD_DIM)
    cols = [slice(h * MOBA_HEAD_DIM, (h + 1) * MOBA_HEAD_DIM) for h in heads]

    k16 = [k_ref[:, c] for c in cols]
    ones = jnp.ones((BF16_SUBLANES, seq), BF16)
    vt_ext = [jnp.concatenate([v_ref[:, c].astype(F32).T.astype(BF16), ones], axis=0) for c in cols]
    km16 = []
    for k in k16:
        k_mean = jnp.mean(k.astype(F32).reshape(n_blk, blk, MOBA_HEAD_DIM), axis=1)
        km16.append(jnp.concatenate(_split_bf16(k_mean), axis=0))

    blk_row = lax.broadcasted_iota(jnp.int32, (n_blk, blk), 0)
    key_in_blk = lax.broadcasted_iota(jnp.int32, (blk, blk), 0)
    qry_in_blk = lax.broadcasted_iota(jnp.int32, (blk, blk), 1)
    causal = key_in_blk <= qry_in_blk

    def run(chains):
        st, gate, m, acc = {}, {}, {}, {}
        for ch in chains:
            h, i = ch
            q16 = q_ref[i * blk:(i + 1) * blk, cols[h]]
            st[ch] = _dot_nt(k16[h][:(i + 1) * blk], q16)
            if i > MOBA_TOPK:
                g2 = _dot_nt(km16[h], q16)
                gate[ch] = g2[:n_blk] + g2[n_blk:]
            m[ch] = jnp.full((1, blk), jnp.finfo(F32).min, F32)
            acc[ch] = None
        for j in range(max(i for _, i in chains) + 1):
            live = [ch for ch in chains if j <= ch[1]]
            p, m_new = {}, {}
            for ch in live:
                h, i = ch
                s = st[ch][j * blk:(j + 1) * blk]
                if j == i:
                    s = jnp.where(causal, s, -jnp.inf)
                col_max = jnp.max(s, axis=0, keepdims=True)
                if i > MOBA_TOPK and j < i:
                    gj = gate[ch][j:j + 1, :]
                    beats = ((gate[ch] > gj) | ((gate[ch] == gj) & (blk_row < j))) & (blk_row < i)
                    keep = jnp.sum(beats.astype(F32), axis=0, keepdims=True) < MOBA_TOPK
                    m_new[ch] = jnp.maximum(m[ch], jnp.where(keep, col_max, -jnp.inf))
                    shift = jnp.where(keep, m_new[ch], jnp.inf)
                else:
                    m_new[ch] = jnp.maximum(m[ch], col_max)
                    shift = m_new[ch]
                p[ch] = jnp.exp2(s - shift).astype(BF16)
            acc_j = {ch: _dot(vt_ext[ch[0]][:, j * blk:(j + 1) * blk], p[ch]) for ch in live}
            for ch in live:
                acc[ch] = acc_j[ch] if j == 0 else jnp.exp2(m[ch] - m_new[ch]) * acc[ch] + acc_j[ch]
                m[ch] = m_new[ch]
        for h, i in chains:
            out = acc[h, i][:MOBA_HEAD_DIM] / acc[h, i][MOBA_HEAD_DIM:MOBA_HEAD_DIM + 1]
            o_ref[i * blk:(i + 1) * blk, cols[h]] = out.T.astype(o_ref.dtype)

    for i in range(n_blk // 2):
        run([(h, ib) for h in heads for ib in (n_blk - 1 - i, i)])


def _moba_attn(qkv, batch, seq, d, *, heads_per_step=4):
    groups = d // (MOBA_HEAD_DIM * heads_per_step)
    blk = (seq, MOBA_HEAD_DIM * heads_per_step)
    return pl.pallas_call(
        _moba_attn_kernel,
        grid=(batch, groups),
        in_specs=[
            pl.BlockSpec(blk, lambda b, h: (b, h)),
            pl.BlockSpec(blk, lambda b, h: (b, groups + h)),
            pl.BlockSpec(blk, lambda b, h: (b, 2 * groups + h)),
        ],
        out_specs=pl.BlockSpec(blk, lambda b, h: (b, h)),
        out_shape=jax.ShapeDtypeStruct((batch * seq, d), BF16),
        compiler_params=_params("parallel", "parallel"),
        name="moba_attn",
    )(qkv, qkv, qkv)


def _ret_proj_kernel(xn_ref, w_ref, cos_ref, sin_ref, o_ref, wb_ref, *, n_q_tiles):
    j = pl.program_id(0)
    half = RET_HEAD_DIM // 2

    @pl.when(pl.program_id(1) == 0)
    def _():
        wb_ref[...] = w_ref[...].astype(BF16)

    def rotate(scale):
        y = _dot(xn_ref[...], wb_ref[...])
        cos, sin = cos_ref[...], sin_ref[...]
        for h in range(y.shape[1] // RET_HEAD_DIM):
            lo = h * RET_HEAD_DIM
            x1, x2 = y[:, lo:lo + half], y[:, lo + half:lo + RET_HEAD_DIM]
            o_ref[:, lo:lo + half] = ((x1 * cos - x2 * sin) * scale).astype(o_ref.dtype)
            o_ref[:, lo + half:lo + RET_HEAD_DIM] = ((x1 * sin + x2 * cos) * scale).astype(o_ref.dtype)

    pl.when(j < n_q_tiles)(lambda: rotate(1.0))
    pl.when((j >= n_q_tiles) & (j < 2 * n_q_tiles))(lambda: rotate(RET_HEAD_DIM ** -0.5))

    @pl.when(j >= 2 * n_q_tiles)
    def _():
        o_ref[...] = _dot(xn_ref[...], wb_ref[...]).astype(o_ref.dtype)


def _ret_proj(xn, w_in, layer, seq, *, tm=1024, tn=1024):
    t, d = xn.shape
    n = w_in.shape[2]
    tn = min(tn, d)
    half = RET_HEAD_DIM // 2
    inv_freq = np.power(np.float32(RET_ROT_BASE), -np.linspace(0.0, 1.0, half, dtype=np.float32))
    ang = np.arange(seq, dtype=np.float32)[:, None] * inv_freq[None, :]
    pos_tiles = seq // tm
    tab_spec = pl.BlockSpec((tm, half), lambda j, i: (i % pos_tiles, 0))
    return pl.pallas_call(
        functools.partial(_ret_proj_kernel, n_q_tiles=d // tn),
        grid=(n // tn, t // tm),
        in_specs=[
            pl.BlockSpec((tm, d), lambda j, i: (i, 0)),
            pl.BlockSpec((None, d, tn), lambda j, i: (layer, 0, j)),
            tab_spec, tab_spec,
        ],
        out_specs=pl.BlockSpec((tm, tn), lambda j, i: (i, j)),
        out_shape=jax.ShapeDtypeStruct((t, n), BF16),
        scratch_shapes=[pltpu.VMEM((d, tn), BF16)],
        compiler_params=_params("arbitrary", "arbitrary"),
        name="ret_proj",
    )(xn, w_in, np.cos(ang), np.sin(ang))


def _ret_kernel(q_ref, k_ref, v_ref, g_ref, gn_ref, dmask_ref, qdec_ref, kdec_ref, cdec_ref, wo_ref, x_ref,
                o_ref, state_ref, wb_ref):
    n = pl.program_id(1)

    @pl.when((pl.program_id(0) == 0) & (n == 0))
    def _():
        wb_ref[...] = wo_ref[...].astype(BF16)

    @pl.when(n == 0)
    def _():
        state_ref[...] = jnp.zeros_like(state_ref)

    heads = range(state_ref.shape[0])
    cols = [slice(h * RET_HEAD_DIM, (h + 1) * RET_HEAD_DIM) for h in heads]
    qc = [q_ref[:, c] for c in cols]
    kc = [k_ref[:, c] for c in cols]
    vc = [v_ref[:, c] for c in cols]
    s = [(_dot_nt(qc[h], kc[h]) * dmask_ref[h]).astype(BF16) for h in heads]
    cross = [_dot(qc[h], state_ref[h].astype(BF16)) * qdec_ref[h] for h in heads]
    out = [_dot(s[h], vc[h]) + cross[h] for h in heads]
    for h in heads:
        kt = (kc[h].astype(F32) * kdec_ref[h]).T.astype(BF16)
        state_ref[h] = state_ref[h] * cdec_ref[h] + _dot(kt, vc[h])
    for h in heads:
        o = out[h]
        o = o * lax.rsqrt(jnp.mean(o * o, axis=-1, keepdims=True) + RMS_EPS) * gn_ref[:, cols[h]]
        y = (_silu(g_ref[:, cols[h]].astype(F32)) * o).astype(BF16)
        o_ref[...] = (x_ref[...] if h == 0 else o_ref[...]) + _dot(y, wb_ref[cols[h], :])


def _retention(proj, gn_gain, w_o, layer, x, batch, seq):
    t, d = x.shape
    heads = d // RET_HEAD_DIM
    c = RET_CHUNK
    n_chunks = seq // c
    f32 = np.float32
    log_gamma = np.log1p(-np.power(f32(2.0), f32(-5.0) - np.arange(heads, dtype=f32)))
    n = np.arange(c, dtype=f32)
    diff = n[:, None] - n[None, :]
    dmask = np.exp(np.where((diff >= 0)[None], diff[None] * log_gamma[:, None, None], f32(-np.inf)))
    qdec = np.exp((n[None, :] + f32(1.0)) * log_gamma[:, None])[:, :, None]
    kdec = np.exp((f32(c - 1.0) - n[None, :]) * log_gamma[:, None])[:, :, None]
    cdec = np.ascontiguousarray(np.broadcast_to(np.exp(f32(c) * log_gamma)[:, None, None], (heads, 1, RET_HEAD_DIM)))

    def rows(col):
        return pl.BlockSpec((c, d), lambda b, n: (b * n_chunks + n, col))

    def whole(a):
        return pl.BlockSpec(a.shape, lambda b, n: (0,) * a.ndim)

    return pl.pallas_call(
        _ret_kernel,
        grid=(batch, n_chunks),
        in_specs=[
            rows(0), rows(1), rows(2), rows(3),
            pl.BlockSpec((None, 1, d), lambda b, n: (layer, 0, 0)),
            whole(dmask), whole(qdec), whole(kdec), whole(cdec),
            pl.BlockSpec((None, d, d), lambda b, n: (layer, 0, 0), pipeline_mode=pl.Buffered(1)),
            rows(0),
        ],
        out_specs=rows(0),
        out_shape=jax.ShapeDtypeStruct((t, d), F32),
        scratch_shapes=[pltpu.VMEM((heads, RET_HEAD_DIM, RET_HEAD_DIM), F32), pltpu.VMEM((d, d), BF16)],
        compiler_params=_params("arbitrary", "arbitrary"),
        name="retention",
    )(proj, proj, proj, proj, gn_gain.reshape(gn_gain.shape[0], 1, d), dmask, qdec, kdec, cdec, w_o, x)


def kernel(x, norm_gain, ffn_w_gate_up, ffn_w_down, moba_w_qkv, moba_w_o, ret_w_in, ret_w_o, ret_gn_gain, final_norm):
    batch, seq, d = x.shape
    depth = norm_gain.shape[0]
    h = x.reshape(batch * seq, d)
    hn = None
    for i in range(depth):
        g = norm_gain[i]
        h, hn = _ffn(h, g[0], ffn_w_gate_up, ffn_w_down, i, 0, g[1], epilogue="norm_out", in_place=i > 0, spare=hn)
        if i % 2 == 0:
            qkv = _moba_proj(hn, moba_w_qkv, i // 2, seq)
            h = _out_proj(_moba_attn(qkv, batch, seq, d), moba_w_o, i // 2, h)
        else:
            proj = _ret_proj(hn, ret_w_in, i // 2, seq)
            h = _retention(proj, ret_gn_gain, ret_w_o, i // 2, h, batch, seq)
        h = _ffn(h, g[2], ffn_w_gate_up, ffn_w_down, i, 1, final_norm,
                 epilogue="final_norm" if i == depth - 1 else "none")[0]
    return h.reshape(batch, seq, d)
```

```python
import functools

import jax
import jax.numpy as jnp
import numpy as np
from jax import lax
from jax.experimental import pallas as pl
from jax.experimental.pallas import tpu as pltpu

FFN_RES = 0.5
RMS_EPS = 1e-6

MOBA_HEAD_DIM = 128
MOBA_BLOCK = 256
MOBA_TOPK = 3
ROPE_THETA = 500000.0
ROPE_DIM = MOBA_HEAD_DIM // 4
ROTARY_BAND_ROWS = 256

RET_HEAD_DIM = 256
RET_CHUNK = 256
RET_ROT_BASE = 10000.0

V7X_VMEM_BYTES = 64 * 1024 * 1024
BF16_SUBLANES = 16
VMEM_LIMIT_BYTES = V7X_VMEM_BYTES - 8 * 1024 * 1024

F32 = jnp.float32
BF16 = jnp.bfloat16


def _params(*semantics):
    return pltpu.CompilerParams(dimension_semantics=semantics, vmem_limit_bytes=VMEM_LIMIT_BYTES)


def _rms_norm(x, gain):
    return x * lax.rsqrt(jnp.mean(x * x, axis=-1, keepdims=True) + RMS_EPS) * gain


def _dot(a, b):
    return jnp.dot(a, b, preferred_element_type=F32)


def _dot_nt(a, b):
    return lax.dot_general(a, b, (((1,), (1,)), ((), ())), preferred_element_type=F32)


def _silu(g):
    return g * jax.nn.sigmoid(g)


def _ffn_step(j, last_j, x_ref, gain_ref, egain_ref, o_ref, xn_out_ref, xn_ref, weights, epilogue):
    def step(first, last):
        if first:
            xn = _rms_norm(x_ref[...], gain_ref[...]).astype(BF16)
            xn_ref[...] = xn
        else:
            xn = xn_ref[...]
        tiles = weights()
        hidden = []
        for wgu, wd in tiles:
            tf = wd.shape[0]
            gu = _dot(xn, wgu)
            hidden.append((_silu(gu[:, :tf]) * gu[:, tf:]).astype(BF16))
        acc = _dot(jnp.concatenate(hidden, axis=1), jnp.concatenate([wd for _, wd in tiles], axis=0))
        if not first:
            acc = o_ref[...] + acc
        if last:
            y = x_ref[...] + FFN_RES * acc
            if epilogue == "final_norm":
                y = _rms_norm(y, egain_ref[...])
            elif epilogue == "norm_out":
                xn_out_ref[...] = _rms_norm(y, egain_ref[...]).astype(BF16)
            o_ref[...] = y
        else:
            o_ref[...] = acc

    pl.when(j == 0)(lambda: step(True, False))
    pl.when((j > 0) & (j < last_j))(lambda: step(False, False))
    pl.when(j == last_j)(lambda: step(False, True))


def _ffn_head_kernel(x_ref, gain_ref, wg_ref, wu_ref, wd_ref, egain_ref, *rest, epilogue, n_unused):
    rest = rest[n_unused:]
    if epilogue == "norm_out":
        o_ref, xn_out_ref, wgu16_ref, wd16_ref, xn_ref = rest
    else:
        o_ref, wgu16_ref, wd16_ref, xn_ref = rest
        xn_out_ref = None
    tf = wd_ref.shape[0]

    def weights():
        wgu16_ref[:, :tf] = wg_ref[...].astype(BF16)
        wgu16_ref[:, tf:] = wu_ref[...].astype(BF16)
        wd16_ref[...] = wd_ref[...].astype(BF16)
        return [(wgu16_ref[...], wd16_ref[...])]

    _ffn_step(pl.program_id(0), pl.num_programs(0) - 1, x_ref, gain_ref, egain_ref, o_ref, xn_out_ref, xn_ref,
              weights, epilogue)


def _ffn_tail_kernel(x_ref, gain_ref, wgu16_ref, wd16_ref, egain_ref, *rest, epilogue, n_unused, skip_first_tile):
    rest = rest[n_unused:]
    if epilogue == "norm_out":
        o_ref, xn_out_ref = rest
        xn_ref = xn_out_ref
    else:
        o_ref, xn_ref = rest
        xn_out_ref = None

    def steps():
        _ffn_step(pl.program_id(1), pl.num_programs(1) - 1, x_ref, gain_ref, egain_ref, o_ref, xn_out_ref, xn_ref,
                  lambda: [(wgu16_ref[s], wd16_ref[s]) for s in range(wd16_ref.shape[0])], epilogue)

    if skip_first_tile:
        i = pl.program_id(0)

        @pl.when((i == 0) & (pl.program_id(1) == 0))
        def _():
            o_ref[...] = jnp.zeros_like(o_ref)
            if xn_out_ref is not None:
                xn_out_ref[...] = jnp.zeros_like(xn_out_ref)

        pl.when(i > 0)(steps)
    else:
        steps()


def _ffn(x, gain, w_gate_up, w_down, layer, which, egain, *, epilogue, in_place=True, spare=None,
         tm=1024, tf=256, tail_tiles=2):
    t, d = x.shape
    f = w_down.shape[2]
    nf = f // tf
    assert nf % tail_tiles == 0 and nf // tail_tiles >= 2
    norm_out = epilogue == "norm_out"
    once = pl.Buffered(1)
    any_spec = pl.BlockSpec(memory_space=pl.ANY)
    n_act = 2 if norm_out else 1
    full_shapes = [jax.ShapeDtypeStruct((t, d), F32), jax.ShapeDtypeStruct((t, d), BF16)][:n_act]
    tile_shapes = [jax.ShapeDtypeStruct((tm, d), F32), jax.ShapeDtypeStruct((tm, d), BF16)][:n_act]
    norm_buf = [jnp.zeros((t, d), BF16) if spare is None else spare] if norm_out and in_place else []

    first_rows = pl.BlockSpec((tm, d), lambda j: (0, 0), pipeline_mode=once)
    head = pl.pallas_call(
        functools.partial(_ffn_head_kernel, epilogue=epilogue, n_unused=len(norm_buf)),
        grid=(nf,),
        in_specs=[
            first_rows,
            pl.BlockSpec((1, d), lambda j: (0, 0)),
            pl.BlockSpec((None, None, d, tf), lambda j: (layer, which, 0, j)),
            pl.BlockSpec((None, None, d, tf), lambda j: (layer, which, 0, j + nf)),
            pl.BlockSpec((None, None, tf, d), lambda j: (layer, which, j, 0)),
            pl.BlockSpec((1, d), lambda j: (0, 0)),
        ] + [any_spec] * len(norm_buf),
        out_specs=[first_rows] * n_act + [
            pl.BlockSpec((None, d, 2 * tf), lambda j: (j, 0, 0)),
            pl.BlockSpec((None, tf, d), lambda j: (j, 0, 0)),
        ],
        out_shape=(full_shapes if in_place else tile_shapes) + [jax.ShapeDtypeStruct((nf, d, 2 * tf), BF16),
                                                                jax.ShapeDtypeStruct((nf, tf, d), BF16)],
        input_output_aliases=({0: 0, 6: 1} if norm_out else {0: 0}) if in_place else {},
        scratch_shapes=[pltpu.VMEM((tm, d), BF16)],
        compiler_params=_params("arbitrary"),
        name="ffn_head",
    )(x, gain.reshape(1, d), w_gate_up, w_gate_up, w_down, egain.reshape(1, d), *norm_buf)
    *acts, wgu16, wd16 = head

    first = 1 if in_place else 0

    def rows(**kw):
        return pl.BlockSpec((tm, d), lambda i, j: (i + first, 0), **kw)

    def weight_tile(i, j):
        return (j if in_place else jnp.where(i == 0, 0, j), 0, 0)

    act = tm * d
    fixed_bytes = (act * 2 + 2 * act * 4
                   + 2 * tail_tiles * 3 * d * tf * 2
                   + tm * tf * (2 * 4 + 2))
    x_mode = {} if fixed_bytes + 2 * act * 4 <= VMEM_LIMIT_BYTES else {"pipeline_mode": once}

    tail = pl.pallas_call(
        functools.partial(_ffn_tail_kernel, epilogue=epilogue, n_unused=len(norm_buf), skip_first_tile=not in_place),
        grid=(t // tm - first, nf // tail_tiles),
        in_specs=[
            rows(**x_mode),
            pl.BlockSpec((1, d), lambda i, j: (0, 0)),
            pl.BlockSpec((tail_tiles, d, 2 * tf), weight_tile),
            pl.BlockSpec((tail_tiles, tf, d), weight_tile),
            pl.BlockSpec((1, d), lambda i, j: (0, 0)),
        ] + [any_spec] * len(norm_buf),
        out_specs=[rows()] + [rows(pipeline_mode=once)] * norm_out,
        out_shape=full_shapes,
        input_output_aliases=({0: 0, 5: 1} if norm_out else {0: 0}) if in_place else {},
        scratch_shapes=[] if norm_out else [pltpu.VMEM((tm, d), BF16)],
        compiler_params=_params("parallel", "arbitrary"),
        name="ffn_tail",
    )(acts[0] if in_place else x, gain.reshape(1, d), wgu16, wd16, egain.reshape(1, d),
      *(acts[1:] if in_place else []))
    if in_place:
        return tail
    return [lax.dynamic_update_slice(full, tile, (0, 0)) for full, tile in zip(tail, acts)]


def _out_proj_kernel(a_ref, w_ref, x_ref, o_ref, wb_ref):
    @pl.when(pl.program_id(0) == 0)
    def _():
        wb_ref[...] = w_ref[...].astype(BF16)

    o_ref[...] = x_ref[...] + _dot(a_ref[...], wb_ref[...])


def _out_proj(a, w, layer, x, *, tm=512):
    t, k = a.shape
    n = w.shape[2]
    return pl.pallas_call(
        _out_proj_kernel,
        grid=(t // tm,),
        in_specs=[
            pl.BlockSpec((tm, k), lambda i: (i, 0)),
            pl.BlockSpec((None, k, n), lambda i: (layer, 0, 0), pipeline_mode=pl.Buffered(1)),
            pl.BlockSpec((tm, n), lambda i: (i, 0)),
        ],
        out_specs=pl.BlockSpec((tm, n), lambda i: (i, 0)),
        out_shape=jax.ShapeDtypeStruct((t, n), F32),
        scratch_shapes=[pltpu.VMEM((k, n), BF16)],
        compiler_params=_params("arbitrary"),
        name="out_proj",
    )(a, w, x)


def _moba_rope_tables(seq):
    half = ROPE_DIM // 2
    f32 = np.float32
    inv_freq = np.power(f32(ROPE_THETA), -np.arange(half, dtype=f32) / f32(half))
    ang = np.arange(seq, dtype=f32)[:, None] * inv_freq[None, :]
    cos, sin = np.cos(ang), np.sin(ang)
    rest = MOBA_HEAD_DIM - ROPE_DIM
    c = np.concatenate([cos, cos, np.ones((seq, rest), f32)], axis=-1)
    a = np.concatenate([-sin, np.zeros((seq, half + rest), f32)], axis=-1)
    b = np.concatenate([np.zeros((seq, half), f32), sin, np.zeros((seq, rest), f32)], axis=-1)
    scale = f32(MOBA_HEAD_DIM ** -0.5) * np.log2(f32(np.e))
    return tuple(np.stack([t * scale, t]) for t in (c, a, b))


def _moba_proj_kernel(xn_ref, w_ref, c_ref, a_ref, b_ref, o_ref, wb_ref, *, n_rot_tiles):
    j = pl.program_id(0)

    @pl.when(pl.program_id(1) == 0)
    def _():
        wb_ref[...] = w_ref[...].astype(BF16)

    @pl.when(j < n_rot_tiles)
    def _():
        half = ROPE_DIM // 2
        for r in range(0, o_ref.shape[0], ROTARY_BAND_ROWS):
            band = slice(r, r + ROTARY_BAND_ROWS)
            y = _dot(xn_ref[band, :], wb_ref[...])
            c, a, b = c_ref[band, :], a_ref[band, :], b_ref[band, :]
            for h in range(y.shape[1] // MOBA_HEAD_DIM):
                sl = slice(h * MOBA_HEAD_DIM, (h + 1) * MOBA_HEAD_DIM)
                yh = y[:, sl]
                up = pltpu.roll(yh, MOBA_HEAD_DIM - half, 1)
                down = pltpu.roll(yh, half, 1)
                o_ref[band, sl] = (yh * c + up * a + down * b).astype(o_ref.dtype)

    @pl.when(j >= n_rot_tiles)
    def _():
        o_ref[...] = _dot(xn_ref[...], wb_ref[...]).astype(o_ref.dtype)


def _moba_proj(xn, w_qkv, layer, seq, *, tm=1024, tn=1024):
    t, d = xn.shape
    n = w_qkv.shape[2]
    tn = min(tn, d)
    c, a, b = _moba_rope_tables(seq)
    pos_tiles = seq // tm
    q_tiles = d // tn
    tab_spec = pl.BlockSpec((None, tm, MOBA_HEAD_DIM),
                            lambda j, i: (jnp.minimum(j // q_tiles, 1), i % pos_tiles, 0))
    return pl.pallas_call(
        functools.partial(_moba_proj_kernel, n_rot_tiles=2 * q_tiles),
        grid=(n // tn, t // tm),
        in_specs=[
            pl.BlockSpec((tm, d), lambda j, i: (i, 0)),
            pl.BlockSpec((None, d, tn), lambda j, i: (layer, 0, j)),
            tab_spec, tab_spec, tab_spec,
        ],
        out_specs=pl.BlockSpec((tm, tn), lambda j, i: (i, j)),
        out_shape=jax.ShapeDtypeStruct((t, n), BF16),
        scratch_shapes=[pltpu.VMEM((d, tn), BF16)],
        compiler_params=_params("arbitrary", "arbitrary"),
        name="moba_proj",
    )(xn, w_qkv, c, a, b)


def _split_bf16(x):
    hi = x.astype(BF16)
    lo = (x - hi.astype(F32)).astype(BF16)
    return hi, lo


def _moba_attn_kernel(q_ref, k_ref, v_ref, o_ref):
    seq = q_ref.shape[0]
    blk = MOBA_BLOCK
    n_blk = seq // blk
    heads = range(q_ref.shape[1] // MOBA_HEAD_DIM)
    cols = [slice(h * MOBA_HEAD_DIM, (h + 1) * MOBA_HEAD_DIM) for h in heads]

    k16 = [k_ref[:, c] for c in cols]
    ones = jnp.ones((BF16_SUBLANES, seq), BF16)
    vt_ext = [jnp.concatenate([v_ref[:, c].astype(F32).T.astype(BF16), ones], axis=0) for c in cols]
    km16 = []
    for k in k16:
        k_mean = jnp.mean(k.astype(F32).reshape(n_blk, blk, MOBA_HEAD_DIM), axis=1)
        km16.append(jnp.concatenate(_split_bf16(k_mean), axis=0))

    blk_row = lax.broadcasted_iota(jnp.int32, (n_blk, blk), 0)
    key_in_blk = lax.broadcasted_iota(jnp.int32, (blk, blk), 0)
    qry_in_blk = lax.broadcasted_iota(jnp.int32, (blk, blk), 1)
    causal = key_in_blk <= qry_in_blk

    def run(chains):
        st, gate, m, acc = {}, {}, {}, {}
        for ch in chains:
            h, i = ch
            q16 = q_ref[i * blk:(i + 1) * blk, cols[h]]
            st[ch] = _dot_nt(k16[h][:(i + 1) * blk], q16)
            if i > MOBA_TOPK:
                g2 = _dot_nt(km16[h], q16)
                gate[ch] = g2[:n_blk] + g2[n_blk:]
            m[ch] = jnp.full((1, blk), jnp.finfo(F32).min, F32)
            acc[ch] = None
        for j in range(max(i for _, i in chains) + 1):
            live = [ch for ch in chains if j <= ch[1]]
            p, m_new = {}, {}
            for ch in live:
                h, i = ch
                s = st[ch][j * blk:(j + 1) * blk]
                if j == i:
                    s = jnp.where(causal, s, -jnp.inf)
                col_max = jnp.max(s, axis=0, keepdims=True)
                if i > MOBA_TOPK and j < i:
                    gj = gate[ch][j:j + 1, :]
                    beats = ((gate[ch] > gj) | ((gate[ch] == gj) & (blk_row < j))) & (blk_row < i)
                    keep = jnp.sum(beats.astype(F32), axis=0, keepdims=True) < MOBA_TOPK
                    m_new[ch] = jnp.maximum(m[ch], jnp.where(keep, col_max, -jnp.inf))
                    shift = jnp.where(keep, m_new[ch], jnp.inf)
                else:
                    m_new[ch] = jnp.maximum(m[ch], col_max)
                    shift = m_new[ch]
                p[ch] = jnp.exp2(s - shift).astype(BF16)
            acc_j = {ch: _dot(vt_ext[ch[0]][:, j * blk:(j + 1) * blk], p[ch]) for ch in live}
            for ch in live:
                acc[ch] = acc_j[ch] if j == 0 else jnp.exp2(m[ch] - m_new[ch]) * acc[ch] + acc_j[ch]
                m[ch] = m_new[ch]
        for h, i in chains:
            out = acc[h, i][:MOBA_HEAD_DIM] / acc[h, i][MOBA_HEAD_DIM:MOBA_HEAD_DIM + 1]
            o_ref[i * blk:(i + 1) * blk, cols[h]] = out.T.astype(o_ref.dtype)

    for i in range(n_blk // 2):
        run([(h, ib) for h in heads for ib in (n_blk - 1 - i, i)])


def _moba_attn(qkv, batch, seq, d, *, heads_per_step=4):
    groups = d // (MOBA_HEAD_DIM * heads_per_step)
    blk = (seq, MOBA_HEAD_DIM * heads_per_step)
    return pl.pallas_call(
        _moba_attn_kernel,
        grid=(batch, groups),
        in_specs=[
            pl.BlockSpec(blk, lambda b, h: (b, h)),
            pl.BlockSpec(blk, lambda b, h: (b, groups + h)),
            pl.BlockSpec(blk, lambda b, h: (b, 2 * groups + h)),
        ],
        out_specs=pl.BlockSpec(blk, lambda b, h: (b, h)),
        out_shape=jax.ShapeDtypeStruct((batch * seq, d), BF16),
        compiler_params=_params("parallel", "parallel"),
        name="moba_attn",
    )(qkv, qkv, qkv)


def _ret_proj_kernel(xn_ref, w_ref, cos_ref, sin_ref, o_ref, wb_ref, *, n_q_tiles):
    j = pl.program_id(0)
    half = RET_HEAD_DIM // 2

    @pl.when(pl.program_id(1) == 0)
    def _():
        wb_ref[...] = w_ref[...].astype(BF16)

    def rotate(scale):
        y = _dot(xn_ref[...], wb_ref[...])
        cos, sin = cos_ref[...], sin_ref[...]
        for h in range(y.shape[1] // RET_HEAD_DIM):
            lo = h * RET_HEAD_DIM
            x1, x2 = y[:, lo:lo + half], y[:, lo + half:lo + RET_HEAD_DIM]
            o_ref[:, lo:lo + half] = ((x1 * cos - x2 * sin) * scale).astype(o_ref.dtype)
            o_ref[:, lo + half:lo + RET_HEAD_DIM] = ((x1 * sin + x2 * cos) * scale).astype(o_ref.dtype)

    pl.when(j < n_q_tiles)(lambda: rotate(1.0))
    pl.when((j >= n_q_tiles) & (j < 2 * n_q_tiles))(lambda: rotate(RET_HEAD_DIM ** -0.5))

    @pl.when(j >= 2 * n_q_tiles)
    def _():
        o_ref[...] = _dot(xn_ref[...], wb_ref[...]).astype(o_ref.dtype)


def _ret_proj(xn, w_in, layer, seq, *, tm=1024, tn=1024):
    t, d = xn.shape
    n = w_in.shape[2]
    tn = min(tn, d)
    half = RET_HEAD_DIM // 2
    inv_freq = np.power(np.float32(RET_ROT_BASE), -np.linspace(0.0, 1.0, half, dtype=np.float32))
    ang = np.arange(seq, dtype=np.float32)[:, None] * inv_freq[None, :]
    pos_tiles = seq // tm
    tab_spec = pl.BlockSpec((tm, half), lambda j, i: (i % pos_tiles, 0))
    return pl.pallas_call(
        functools.partial(_ret_proj_kernel, n_q_tiles=d // tn),
        grid=(n // tn, t // tm),
        in_specs=[
            pl.BlockSpec((tm, d), lambda j, i: (i, 0)),
            pl.BlockSpec((None, d, tn), lambda j, i: (layer, 0, j)),
            tab_spec, tab_spec,
        ],
        out_specs=pl.BlockSpec((tm, tn), lambda j, i: (i, j)),
        out_shape=jax.ShapeDtypeStruct((t, n), BF16),
        scratch_shapes=[pltpu.VMEM((d, tn), BF16)],
        compiler_params=_params("arbitrary", "arbitrary"),
        name="ret_proj",
    )(xn, w_in, np.cos(ang), np.sin(ang))


def _ret_kernel(q_ref, k_ref, v_ref, g_ref, gn_ref, dmask_ref, qdec_ref, kdec_ref, cdec_ref, wo_ref, x_ref,
                o_ref, state_ref, wb_ref):
    n = pl.program_id(1)

    @pl.when((pl.program_id(0) == 0) & (n == 0))
    def _():
        wb_ref[...] = wo_ref[...].astype(BF16)

    @pl.when(n == 0)
    def _():
        state_ref[...] = jnp.zeros_like(state_ref)

    heads = range(state_ref.shape[0])
    cols = [slice(h * RET_HEAD_DIM, (h + 1) * RET_HEAD_DIM) for h in heads]
    qc = [q_ref[:, c] for c in cols]
    kc = [k_ref[:, c] for c in cols]
    vc = [v_ref[:, c] for c in cols]
    s = [(_dot_nt(qc[h], kc[h]) * dmask_ref[h]).astype(BF16) for h in heads]
    cross = [_dot(qc[h], state_ref[h].astype(BF16)) * qdec_ref[h] for h in heads]
    out = [_dot(s[h], vc[h]) + cross[h] for h in heads]
    for h in heads:
        kt = (kc[h].astype(F32) * kdec_ref[h]).T.astype(BF16)
        state_ref[h] = state_ref[h] * cdec_ref[h] + _dot(kt, vc[h])
    for h in heads:
        o = out[h]
        o = o * lax.rsqrt(jnp.mean(o * o, axis=-1, keepdims=True) + RMS_EPS) * gn_ref[:, cols[h]]
        y = (_silu(g_ref[:, cols[h]].astype(F32)) * o).astype(BF16)
        o_ref[...] = (x_ref[...] if h == 0 else o_ref[...]) + _dot(y, wb_ref[cols[h], :])


def _retention(proj, gn_gain, w_o, layer, x, batch, seq):
    t, d = x.shape
    heads = d // RET_HEAD_DIM
    c = RET_CHUNK
    n_chunks = seq // c
    f32 = np.float32
    log_gamma = np.log1p(-np.power(f32(2.0), f32(-5.0) - np.arange(heads, dtype=f32)))
    n = np.arange(c, dtype=f32)
    diff = n[:, None] - n[None, :]
    dmask = np.exp(np.where((diff >= 0)[None], diff[None] * log_gamma[:, None, None], f32(-np.inf)))
    qdec = np.exp((n[None, :] + f32(1.0)) * log_gamma[:, None])[:, :, None]
    kdec = np.exp((f32(c - 1.0) - n[None, :]) * log_gamma[:, None])[:, :, None]
    cdec = np.ascontiguousarray(np.broadcast_to(np.exp(f32(c) * log_gamma)[:, None, None], (heads, 1, RET_HEAD_DIM)))

    def rows(col):
        return pl.BlockSpec((c, d), lambda b, n: (b * n_chunks + n, col))

    def whole(a):
        return pl.BlockSpec(a.shape, lambda b, n: (0,) * a.ndim)

    return pl.pallas_call(
        _ret_kernel,
        grid=(batch, n_chunks),
        in_specs=[
            rows(0), rows(1), rows(2), rows(3),
            pl.BlockSpec((None, 1, d), lambda b, n: (layer, 0, 0)),
            whole(dmask), whole(qdec), whole(kdec), whole(cdec),
            pl.BlockSpec((None, d, d), lambda b, n: (layer, 0, 0), pipeline_mode=pl.Buffered(1)),
            rows(0),
        ],
        out_specs=rows(0),
        out_shape=jax.ShapeDtypeStruct((t, d), F32),
        scratch_shapes=[pltpu.VMEM((heads, RET_HEAD_DIM, RET_HEAD_DIM), F32), pltpu.VMEM((d, d), BF16)],
        compiler_params=_params("arbitrary", "arbitrary"),
        name="retention",
    )(proj, proj, proj, proj, gn_gain.reshape(gn_gain.shape[0], 1, d), dmask, qdec, kdec, cdec, w_o, x)


def kernel(x, norm_gain, ffn_w_gate_up, ffn_w_down, moba_w_qkv, moba_w_o, ret_w_in, ret_w_o, ret_gn_gain, final_norm):
    batch, seq, d = x.shape
    depth = norm_gain.shape[0]
    h = x.reshape(batch * seq, d)
    hn = None
    for i in range(depth):
        g = norm_gain[i]
        h, hn = _ffn(h, g[0], ffn_w_gate_up, ffn_w_down, i, 0, g[1], epilogue="norm_out", in_place=i > 0, spare=hn)
        if i % 2 == 0:
            qkv = _moba_proj(hn, moba_w_qkv, i // 2, seq)
            h = _out_proj(_moba_attn(qkv, batch, seq, d), moba_w_o, i // 2, h)
        else:
            proj = _ret_proj(hn, ret_w_in, i // 2, seq)
            h = _retention(proj, ret_gn_gain, ret_w_o, i // 2, h, batch, seq)
        h = _ffn(h, g[2], ffn_w_gate_up, ffn_w_down, i, 1, final_norm,
                 epilogue="final_norm" if i == depth - 1 else "none")[0]
    return h.reshape(batch, seq, d)
```

```python
import functools

import jax
import jax.numpy as jnp
import numpy as np
from jax import lax
from jax.experimental import pallas as pl
from jax.experimental.pallas import tpu as pltpu

FFN_RES = 0.5
RMS_EPS = 1e-6

MOBA_HEAD_DIM = 128
MOBA_BLOCK = 256
MOBA_TOPK = 3
ROPE_THETA = 500000.0
ROPE_DIM = MOBA_HEAD_DIM // 4
ROTARY_BAND_ROWS = 256

RET_HEAD_DIM = 256
RET_CHUNK = 256
RET_ROT_BASE = 10000.0

V7X_VMEM_BYTES = 64 * 1024 * 1024
BF16_SUBLANES = 16
VMEM_LIMIT_BYTES = V7X_VMEM_BYTES - 8 * 1024 * 1024

F32 = jnp.float32
BF16 = jnp.bfloat16


def _params(*semantics):
    return pltpu.CompilerParams(dimension_semantics=semantics, vmem_limit_bytes=VMEM_LIMIT_BYTES)


def _rms_norm(x, gain):
    return x * lax.rsqrt(jnp.mean(x * x, axis=-1, keepdims=True) + RMS_EPS) * gain


def _dot(a, b):
    return jnp.dot(a, b, preferred_element_type=F32)


def _dot_nt(a, b):
    return lax.dot_general(a, b, (((1,), (1,)), ((), ())), preferred_element_type=F32)


def _silu(g):
    return g * jax.nn.sigmoid(g)


def _ffn_step(j, last_j, x_ref, gain_ref, egain_ref, o_ref, xn_out_ref, xn_ref, weights, epilogue):
    def step(first, last):
        if first:
            xn = _rms_norm(x_ref[...], gain_ref[...]).astype(BF16)
            xn_ref[...] = xn
        else:
            xn = xn_ref[...]
        tiles = weights()
        hidden = []
        for wgu, wd in tiles:
            tf = wd.shape[0]
            gu = _dot(xn, wgu)
            hidden.append((_silu(gu[:, :tf]) * gu[:, tf:]).astype(BF16))
        acc = _dot(jnp.concatenate(hidden, axis=1), jnp.concatenate([wd for _, wd in tiles], axis=0))
        if not first:
            acc = o_ref[...] + acc
        if last:
            y = x_ref[...] + FFN_RES * acc
            if epilogue == "final_norm":
                y = _rms_norm(y, egain_ref[...])
            elif epilogue == "norm_out":
                xn_out_ref[...] = _rms_norm(y, egain_ref[...]).astype(BF16)
            o_ref[...] = y
        else:
            o_ref[...] = acc

    pl.when(j == 0)(lambda: step(True, False))
    pl.when((j > 0) & (j < last_j))(lambda: step(False, False))
    pl.when(j == last_j)(lambda: step(False, True))


def _ffn_head_kernel(x_ref, gain_ref, wg_ref, wu_ref, wd_ref, egain_ref, *rest, epilogue, n_unused):
    rest = rest[n_unused:]
    if epilogue == "norm_out":
        o_ref, xn_out_ref, wgu16_ref, wd16_ref, xn_ref = rest
    else:
        o_ref, wgu16_ref, wd16_ref, xn_ref = rest
        xn_out_ref = None
    tf = wd_ref.shape[0]

    def weights():
        wgu16_ref[:, :tf] = wg_ref[...].astype(BF16)
        wgu16_ref[:, tf:] = wu_ref[...].astype(BF16)
        wd16_ref[...] = wd_ref[...].astype(BF16)
        return [(wgu16_ref[...], wd16_ref[...])]

    _ffn_step(pl.program_id(0), pl.num_programs(0) - 1, x_ref, gain_ref, egain_ref, o_ref, xn_out_ref, xn_ref,
              weights, epilogue)


def _ffn_tail_kernel(x_ref, gain_ref, wgu16_ref, wd16_ref, egain_ref, *rest, epilogue, n_unused, skip_first_tile):
    rest = rest[n_unused:]
    if epilogue == "norm_out":
        o_ref, xn_out_ref = rest
        xn_ref = xn_out_ref
    else:
        o_ref, xn_ref = rest
        xn_out_ref = None

    def steps():
        _ffn_step(pl.program_id(1), pl.num_programs(1) - 1, x_ref, gain_ref, egain_ref, o_ref, xn_out_ref, xn_ref,
                  lambda: [(wgu16_ref[s], wd16_ref[s]) for s in range(wd16_ref.shape[0])], epilogue)

    if skip_first_tile:
        i = pl.program_id(0)

        @pl.when((i == 0) & (pl.program_id(1) == 0))
        def _():
            o_ref[...] = jnp.zeros_like(o_ref)
            if xn_out_ref is not None:
                xn_out_ref[...] = jnp.zeros_like(xn_out_ref)

        pl.when(i > 0)(steps)
    else:
        steps()


def _ffn(x, gain, w_gate_up, w_down, layer, which, egain, *, epilogue, in_place=True, spare=None,
         tm=1024, tf=256, tail_tiles=2):
    t, d = x.shape
    f = w_down.shape[2]
    nf = f // tf
    assert nf % tail_tiles == 0 and nf // tail_tiles >= 2
    norm_out = epilogue == "norm_out"
    once = pl.Buffered(1)
    any_spec = pl.BlockSpec(memory_space=pl.ANY)
    n_act = 2 if norm_out else 1
    full_shapes = [jax.ShapeDtypeStruct((t, d), F32), jax.ShapeDtypeStruct((t, d), BF16)][:n_act]
    tile_shapes = [jax.ShapeDtypeStruct((tm, d), F32), jax.ShapeDtypeStruct((tm, d), BF16)][:n_act]
    norm_buf = [jnp.zeros((t, d), BF16) if spare is None else spare] if norm_out and in_place else []

    first_rows = pl.BlockSpec((tm, d), lambda j: (0, 0), pipeline_mode=once)
    head = pl.pallas_call(
        functools.partial(_ffn_head_kernel, epilogue=epilogue, n_unused=len(norm_buf)),
        grid=(nf,),
        in_specs=[
            first_rows,
            pl.BlockSpec((1, d), lambda j: (0, 0)),
            pl.BlockSpec((None, None, d, tf), lambda j: (layer, which, 0, j)),
            pl.BlockSpec((None, None, d, tf), lambda j: (layer, which, 0, j + nf)),
            pl.BlockSpec((None, None, tf, d), lambda j: (layer, which, j, 0)),
            pl.BlockSpec((1, d), lambda j: (0, 0)),
        ] + [any_spec] * len(norm_buf),
        out_specs=[first_rows] * n_act + [
            pl.BlockSpec((None, d, 2 * tf), lambda j: (j, 0, 0)),
            pl.BlockSpec((None, tf, d), lambda j: (j, 0, 0)),
        ],
        out_shape=(full_shapes if in_place else tile_shapes) + [jax.ShapeDtypeStruct((nf, d, 2 * tf), BF16),
                                                                jax.ShapeDtypeStruct((nf, tf, d), BF16)],
        input_output_aliases=({0: 0, 6: 1} if norm_out else {0: 0}) if in_place else {},
        scratch_shapes=[pltpu.VMEM((tm, d), BF16)],
        compiler_params=_params("arbitrary"),
        name="ffn_head",
    )(x, gain.reshape(1, d), w_gate_up, w_gate_up, w_down, egain.reshape(1, d), *norm_buf)
    *acts, wgu16, wd16 = head

    first = 1 if in_place else 0

    def rows(**kw):
        return pl.BlockSpec((tm, d), lambda i, j: (i + first, 0), **kw)

    def weight_tile(i, j):
        return (j if in_place else jnp.where(i == 0, 0, j), 0, 0)

    act = tm * d
    fixed_bytes = (act * 2 + 2 * act * 4
                   + 2 * tail_tiles * 3 * d * tf * 2
                   + tm * tf * (2 * 4 + 2))
    x_mode = {} if fixed_bytes + 2 * act * 4 <= VMEM_LIMIT_BYTES else {"pipeline_mode": once}

    tail = pl.pallas_call(
        functools.partial(_ffn_tail_kernel, epilogue=epilogue, n_unused=len(norm_buf), skip_first_tile=not in_place),
        grid=(t // tm - first, nf // tail_tiles),
        in_specs=[
            rows(**x_mode),
            pl.BlockSpec((1, d), lambda i, j: (0, 0)),
            pl.BlockSpec((tail_tiles, d, 2 * tf), weight_tile),
            pl.BlockSpec((tail_tiles, tf, d), weight_tile),
            pl.BlockSpec((1, d), lambda i, j: (0, 0)),
        ] + [any_spec] * len(norm_buf),
        out_specs=[rows()] + [rows(pipeline_mode=once)] * norm_out,
        out_shape=full_shapes,
        input_output_aliases=({0: 0, 5: 1} if norm_out else {0: 0}) if in_place else {},
        scratch_shapes=[] if norm_out else [pltpu.VMEM((tm, d), BF16)],
        compiler_params=_params("parallel", "arbitrary"),
        name="ffn_tail",
    )(acts[0] if in_place else x, gain.reshape(1, d), wgu16, wd16, egain.reshape(1, d),
      *(acts[1:] if in_place else []))
    if in_place:
        return tail
    return [lax.dynamic_update_slice(full, tile, (0, 0)) for full, tile in zip(tail, acts)]


def _out_proj_kernel(a_ref, w_ref, x_ref, o_ref, wb_ref):
    @pl.when(pl.program_id(0) == 0)
    def _():
        wb_ref[...] = w_ref[...].astype(BF16)

    o_ref[...] = x_ref[...] + _dot(a_ref[...], wb_ref[...])


def _out_proj(a, w, layer, x, *, tm=512):
    t, k = a.shape
    n = w.shape[2]
    return pl.pallas_call(
        _out_proj_kernel,
        grid=(t // tm,),
        in_specs=[
            pl.BlockSpec((tm, k), lambda i: (i, 0)),
            pl.BlockSpec((None, k, n), lambda i: (layer, 0, 0), pipeline_mode=pl.Buffered(1)),
            pl.BlockSpec((tm, n), lambda i: (i, 0)),
        ],
        out_specs=pl.BlockSpec((tm, n), lambda i: (i, 0)),
        out_shape=jax.ShapeDtypeStruct((t, n), F32),
        scratch_shapes=[pltpu.VMEM((k, n), BF16)],
        compiler_params=_params("arbitrary"),
        name="out_proj",
    )(a, w, x)


def _moba_rope_tables(seq):
    half = ROPE_DIM // 2
    f32 = np.float32
    inv_freq = np.power(f32(ROPE_THETA), -np.arange(half, dtype=f32) / f32(half))
    ang = np.arange(seq, dtype=f32)[:, None] * inv_freq[None, :]
    cos, sin = np.cos(ang), np.sin(ang)
    rest = MOBA_HEAD_DIM - ROPE_DIM
    c = np.concatenate([cos, cos, np.ones((seq, rest), f32)], axis=-1)
    a = np.concatenate([-sin, np.zeros((seq, half + rest), f32)], axis=-1)
    b = np.concatenate([np.zeros((seq, half), f32), sin, np.zeros((seq, rest), f32)], axis=-1)
    scale = f32(MOBA_HEAD_DIM ** -0.5) * np.log2(f32(np.e))
    return tuple(np.stack([t * scale, t]) for t in (c, a, b))


def _weight_half_specs(layer, d, tn, n_col, n_tok):
    blk = (None, d // 2, tn)
    top = pl.BlockSpec(blk, lambda j, i: (layer, 0, jnp.minimum(j + (i + 1) // n_tok, n_col - 1)))
    bottom = pl.BlockSpec(blk, lambda j, i: (layer, 1, j))
    return top, bottom


def _cast_weight_halves(w_top_ref, w_bottom_ref, wb_ref):
    half = w_top_ref.shape[0]
    wb_ref[:half, :] = w_top_ref[...].astype(BF16)
    wb_ref[half:, :] = w_bottom_ref[...].astype(BF16)


def _moba_proj_kernel(xn_ref, w_top_ref, w_bottom_ref, c_ref, a_ref, b_ref, o_ref, wb_ref, *, n_rot_tiles):
    j = pl.program_id(0)

    @pl.when(pl.program_id(1) == 0)
    def _():
        _cast_weight_halves(w_top_ref, w_bottom_ref, wb_ref)

    @pl.when(j < n_rot_tiles)
    def _():
        half = ROPE_DIM // 2
        for r in range(0, o_ref.shape[0], ROTARY_BAND_ROWS):
            band = slice(r, r + ROTARY_BAND_ROWS)
            y = _dot(xn_ref[band, :], wb_ref[...])
            c, a, b = c_ref[band, :], a_ref[band, :], b_ref[band, :]
            for h in range(y.shape[1] // MOBA_HEAD_DIM):
                sl = slice(h * MOBA_HEAD_DIM, (h + 1) * MOBA_HEAD_DIM)
                yh = y[:, sl]
                up = pltpu.roll(yh, MOBA_HEAD_DIM - half, 1)
                down = pltpu.roll(yh, half, 1)
                o_ref[band, sl] = (yh * c + up * a + down * b).astype(o_ref.dtype)

    @pl.when(j >= n_rot_tiles)
    def _():
        o_ref[...] = _dot(xn_ref[...], wb_ref[...]).astype(o_ref.dtype)


def _moba_proj(xn, w_qkv, layer, seq, *, tm=1024, tn=1024):
    t, d = xn.shape
    n = w_qkv.shape[2]
    tn = min(tn, d)
    c, a, b = _moba_rope_tables(seq)
    pos_tiles = seq // tm
    q_tiles = d // tn
    tab_spec = pl.BlockSpec((None, tm, MOBA_HEAD_DIM),
                            lambda j, i: (jnp.minimum(j // q_tiles, 1), i % pos_tiles, 0))
    return pl.pallas_call(
        functools.partial(_moba_proj_kernel, n_rot_tiles=2 * q_tiles),
        grid=(n // tn, t // tm),
        in_specs=[
            pl.BlockSpec((tm, d), lambda j, i: (i, 0)),
            *_weight_half_specs(layer, d, tn, n // tn, t // tm),
            tab_spec, tab_spec, tab_spec,
        ],
        out_specs=pl.BlockSpec((tm, tn), lambda j, i: (i, j)),
        out_shape=jax.ShapeDtypeStruct((t, n), BF16),
        scratch_shapes=[pltpu.VMEM((d, tn), BF16)],
        compiler_params=_params("arbitrary", "arbitrary"),
        name="moba_proj",
    )(xn, w_qkv, w_qkv, c, a, b)


def _split_bf16(x):
    hi = x.astype(BF16)
    lo = (x - hi.astype(F32)).astype(BF16)
    return hi, lo


def _moba_attn_kernel(q_ref, k_ref, v_ref, o_ref):
    seq = q_ref.shape[0]
    blk = MOBA_BLOCK
    n_blk = seq // blk
    heads = range(q_ref.shape[1] // MOBA_HEAD_DIM)
    cols = [slice(h * MOBA_HEAD_DIM, (h + 1) * MOBA_HEAD_DIM) for h in heads]

    k16 = [k_ref[:, c] for c in cols]
    ones = jnp.ones((BF16_SUBLANES, seq), BF16)
    vt_ext = [jnp.concatenate([v_ref[:, c].astype(F32).T.astype(BF16), ones], axis=0) for c in cols]
    km16 = []
    for k in k16:
        k_mean = jnp.mean(k.astype(F32).reshape(n_blk, blk, MOBA_HEAD_DIM), axis=1)
        km16.append(jnp.concatenate(_split_bf16(k_mean), axis=0))

    blk_row = lax.broadcasted_iota(jnp.int32, (n_blk, blk), 0)
    key_in_blk = lax.broadcasted_iota(jnp.int32, (blk, blk), 0)
    qry_in_blk = lax.broadcasted_iota(jnp.int32, (blk, blk), 1)
    causal = key_in_blk <= qry_in_blk

    def run(chains):
        st, gate, m, acc = {}, {}, {}, {}
        for ch in chains:
            h, i = ch
            q16 = q_ref[i * blk:(i + 1) * blk, cols[h]]
            st[ch] = _dot_nt(k16[h][:(i + 1) * blk], q16)
            if i > MOBA_TOPK:
                g2 = _dot_nt(km16[h], q16)
                gate[ch] = g2[:n_blk] + g2[n_blk:]
            m[ch] = jnp.full((1, blk), jnp.finfo(F32).min, F32)
            acc[ch] = None
        for j in range(max(i for _, i in chains) + 1):
            live = [ch for ch in chains if j <= ch[1]]
            p, m_new = {}, {}
            for ch in live:
                h, i = ch
                s = st[ch][j * blk:(j + 1) * blk]
                if j == i:
                    s = jnp.where(causal, s, -jnp.inf)
                col_max = jnp.max(s, axis=0, keepdims=True)
                if i > MOBA_TOPK and j < i:
                    gj = gate[ch][j:j + 1, :]
                    beats = ((gate[ch] > gj) | ((gate[ch] == gj) & (blk_row < j))) & (blk_row < i)
                    keep = jnp.sum(beats.astype(F32), axis=0, keepdims=True) < MOBA_TOPK
                    m_new[ch] = jnp.maximum(m[ch], jnp.where(keep, col_max, -jnp.inf))
                    shift = jnp.where(keep, m_new[ch], jnp.inf)
                else:
                    m_new[ch] = jnp.maximum(m[ch], col_max)
                    shift = m_new[ch]
                p[ch] = jnp.exp2(s - shift).astype(BF16)
            acc_j = {ch: _dot(vt_ext[ch[0]][:, j * blk:(j + 1) * blk], p[ch]) for ch in live}
            for ch in live:
                acc[ch] = acc_j[ch] if j == 0 else jnp.exp2(m[ch] - m_new[ch]) * acc[ch] + acc_j[ch]
                m[ch] = m_new[ch]
        for h, i in chains:
            out = acc[h, i][:MOBA_HEAD_DIM] / acc[h, i][MOBA_HEAD_DIM:MOBA_HEAD_DIM + 1]
            o_ref[i * blk:(i + 1) * blk, cols[h]] = out.T.astype(o_ref.dtype)

    for i in range(n_blk // 2):
        run([(h, ib) for h in heads for ib in (n_blk - 1 - i, i)])


def _moba_attn(qkv, batch, seq, d, *, heads_per_step=4):
    groups = d // (MOBA_HEAD_DIM * heads_per_step)
    blk = (seq, MOBA_HEAD_DIM * heads_per_step)
    return pl.pallas_call(
        _moba_attn_kernel,
        grid=(batch, groups),
        in_specs=[
            pl.BlockSpec(blk, lambda b, h: (b, h)),
            pl.BlockSpec(blk, lambda b, h: (b, groups + h)),
            pl.BlockSpec(blk, lambda b, h: (b, 2 * groups + h)),
        ],
        out_specs=pl.BlockSpec(blk, lambda b, h: (b, h)),
        out_shape=jax.ShapeDtypeStruct((batch * seq, d), BF16),
        compiler_params=_params("parallel", "parallel"),
        name="moba_attn",
    )(qkv, qkv, qkv)


def _ret_proj_kernel(xn_ref, w_top_ref, w_bottom_ref, cos_ref, sin_ref, o_ref, wb_ref, *, n_q_tiles):
    j = pl.program_id(0)
    half = RET_HEAD_DIM // 2

    @pl.when(pl.program_id(1) == 0)
    def _():
        _cast_weight_halves(w_top_ref, w_bottom_ref, wb_ref)

    def rotate(scale):
        y = _dot(xn_ref[...], wb_ref[...])
        cos, sin = cos_ref[...], sin_ref[...]
        for h in range(y.shape[1] // RET_HEAD_DIM):
            lo = h * RET_HEAD_DIM
            x1, x2 = y[:, lo:lo + half], y[:, lo + half:lo + RET_HEAD_DIM]
            o_ref[:, lo:lo + half] = ((x1 * cos - x2 * sin) * scale).astype(o_ref.dtype)
            o_ref[:, lo + half:lo + RET_HEAD_DIM] = ((x1 * sin + x2 * cos) * scale).astype(o_ref.dtype)

    pl.when(j < n_q_tiles)(lambda: rotate(1.0))
    pl.when((j >= n_q_tiles) & (j < 2 * n_q_tiles))(lambda: rotate(RET_HEAD_DIM ** -0.5))

    @pl.when(j >= 2 * n_q_tiles)
    def _():
        o_ref[...] = _dot(xn_ref[...], wb_ref[...]).astype(o_ref.dtype)


def _ret_proj(xn, w_in, layer, seq, *, tm=1024, tn=1024):
    t, d = xn.shape
    n = w_in.shape[2]
    tn = min(tn, d)
    half = RET_HEAD_DIM // 2
    inv_freq = np.power(np.float32(RET_ROT_BASE), -np.linspace(0.0, 1.0, half, dtype=np.float32))
    ang = np.arange(seq, dtype=np.float32)[:, None] * inv_freq[None, :]
    pos_tiles = seq // tm
    tab_spec = pl.BlockSpec((tm, half), lambda j, i: (i % pos_tiles, 0))
    return pl.pallas_call(
        functools.partial(_ret_proj_kernel, n_q_tiles=d // tn),
        grid=(n // tn, t // tm),
        in_specs=[
            pl.BlockSpec((tm, d), lambda j, i: (i, 0)),
            *_weight_half_specs(layer, d, tn, n // tn, t // tm),
            tab_spec, tab_spec,
        ],
        out_specs=pl.BlockSpec((tm, tn), lambda j, i: (i, j)),
        out_shape=jax.ShapeDtypeStruct((t, n), BF16),
        scratch_shapes=[pltpu.VMEM((d, tn), BF16)],
        compiler_params=_params("arbitrary", "arbitrary"),
        name="ret_proj",
    )(xn, w_in, w_in, np.cos(ang), np.sin(ang))


def _ret_kernel(q_ref, k_ref, v_ref, g_ref, gn_ref, dmask_ref, qdec_ref, kdec_ref, cdec_ref, wo_ref, x_ref,
                o_ref, state_ref, wb_ref):
    n = pl.program_id(1)

    @pl.when((pl.program_id(0) == 0) & (n == 0))
    def _():
        wb_ref[...] = wo_ref[...].astype(BF16)

    @pl.when(n == 0)
    def _():
        state_ref[...] = jnp.zeros_like(state_ref)

    heads = range(state_ref.shape[0])
    cols = [slice(h * RET_HEAD_DIM, (h + 1) * RET_HEAD_DIM) for h in heads]
    qc = [q_ref[:, c] for c in cols]
    kc = [k_ref[:, c] for c in cols]
    vc = [v_ref[:, c] for c in cols]
    s = [(_dot_nt(qc[h], kc[h]) * dmask_ref[h]).astype(BF16) for h in heads]
    cross = [_dot(qc[h], state_ref[h].astype(BF16)) * qdec_ref[h] for h in heads]
    out = [_dot(s[h], vc[h]) + cross[h] for h in heads]
    for h in heads:
        kt = (kc[h].astype(F32) * kdec_ref[h]).T.astype(BF16)
        state_ref[h] = state_ref[h] * cdec_ref[h] + _dot(kt, vc[h])
    for h in heads:
        o = out[h]
        o = o * lax.rsqrt(jnp.mean(o * o, axis=-1, keepdims=True) + RMS_EPS) * gn_ref[:, cols[h]]
        y = (_silu(g_ref[:, cols[h]].astype(F32)) * o).astype(BF16)
        o_ref[...] = (x_ref[...] if h == 0 else o_ref[...]) + _dot(y, wb_ref[cols[h], :])


def _retention(proj, gn_gain, w_o, layer, x, batch, seq):
    t, d = x.shape
    heads = d // RET_HEAD_DIM
    c = RET_CHUNK
    n_chunks = seq // c
    f32 = np.float32
    log_gamma = np.log1p(-np.power(f32(2.0), f32(-5.0) - np.arange(heads, dtype=f32)))
    n = np.arange(c, dtype=f32)
    diff = n[:, None] - n[None, :]
    dmask = np.exp(np.where((diff >= 0)[None], diff[None] * log_gamma[:, None, None], f32(-np.inf)))
    qdec = np.exp((n[None, :] + f32(1.0)) * log_gamma[:, None])[:, :, None]
    kdec = np.exp((f32(c - 1.0) - n[None, :]) * log_gamma[:, None])[:, :, None]
    cdec = np.ascontiguousarray(np.broadcast_to(np.exp(f32(c) * log_gamma)[:, None, None], (heads, 1, RET_HEAD_DIM)))

    def rows(col):
        return pl.BlockSpec((c, d), lambda b, n: (b * n_chunks + n, col))

    def whole(a):
        return pl.BlockSpec(a.shape, lambda b, n: (0,) * a.ndim)

    return pl.pallas_call(
        _ret_kernel,
        grid=(batch, n_chunks),
        in_specs=[
            rows(0), rows(1), rows(2), rows(3),
            pl.BlockSpec((None, 1, d), lambda b, n: (layer, 0, 0)),
            whole(dmask), whole(qdec), whole(kdec), whole(cdec),
            pl.BlockSpec((None, d, d), lambda b, n: (layer, 0, 0), pipeline_mode=pl.Buffered(1)),
            rows(0),
        ],
        out_specs=rows(0),
        out_shape=jax.ShapeDtypeStruct((t, d), F32),
        scratch_shapes=[pltpu.VMEM((heads, RET_HEAD_DIM, RET_HEAD_DIM), F32), pltpu.VMEM((d, d), BF16)],
        compiler_params=_params("arbitrary", "arbitrary"),
        name="retention",
    )(proj, proj, proj, proj, gn_gain.reshape(gn_gain.shape[0], 1, d), dmask, qdec, kdec, cdec, w_o, x)


def kernel(x, norm_gain, ffn_w_gate_up, ffn_w_down, moba_w_qkv, moba_w_o, ret_w_in, ret_w_o, ret_gn_gain, final_norm):
    batch, seq, d = x.shape
    depth = norm_gain.shape[0]
    h = x.reshape(batch * seq, d)
    hn = None
    for i in range(depth):
        g = norm_gain[i]
        h, hn = _ffn(h, g[0], ffn_w_gate_up, ffn_w_down, i, 0, g[1], epilogue="norm_out", in_place=i > 0, spare=hn)
        if i % 2 == 0:
            qkv = _moba_proj(hn, moba_w_qkv, i // 2, seq)
            h = _out_proj(_moba_attn(qkv, batch, seq, d), moba_w_o, i // 2, h)
        else:
            proj = _ret_proj(hn, ret_w_in, i // 2, seq)
            h = _retention(proj, ret_gn_gain, ret_w_o, i // 2, h, batch, seq)
        h = _ffn(h, g[2], ffn_w_gate_up, ffn_w_down, i, 1, final_norm,
                 epilogue="final_norm" if i == depth - 1 else "none")[0]
    return h.reshape(batch, seq, d)
```

```python
import functools

import jax
import jax.numpy as jnp
import numpy as np
from jax import lax
from jax.experimental import pallas as pl
from jax.experimental.pallas import tpu as pltpu

FFN_RES = 0.5
RMS_EPS = 1e-6

MOBA_HEAD_DIM = 128
MOBA_BLOCK = 256
MOBA_TOPK = 3
ROPE_THETA = 500000.0
ROPE_DIM = MOBA_HEAD_DIM // 4
ROTARY_BAND_ROWS = 256

RET_HEAD_DIM = 256
RET_CHUNK = 256
RET_ROT_BASE = 10000.0

V7X_VMEM_BYTES = 64 * 1024 * 1024
BF16_SUBLANES = 16
VMEM_LIMIT_BYTES = V7X_VMEM_BYTES - 8 * 1024 * 1024

F32 = jnp.float32
BF16 = jnp.bfloat16


def _params(*semantics):
    return pltpu.CompilerParams(dimension_semantics=semantics, vmem_limit_bytes=VMEM_LIMIT_BYTES)


def _rms_norm(x, gain):
    return x * lax.rsqrt(jnp.mean(x * x, axis=-1, keepdims=True) + RMS_EPS) * gain


def _dot(a, b):
    return jnp.dot(a, b, preferred_element_type=F32)


def _dot_nt(a, b):
    return lax.dot_general(a, b, (((1,), (1,)), ((), ())), preferred_element_type=F32)


def _silu(g):
    return g * jax.nn.sigmoid(g)


def _ffn_step(j, last_j, x_ref, gain_ref, egain_ref, o_ref, xn_out_ref, xn_ref, weights, epilogue):
    def step(first, last):
        if first:
            x = x_ref[...]
            xn = _rms_norm(x, gain_ref[...]).astype(BF16)
            xn_ref[...] = xn
        else:
            xn = xn_ref[...]
        tiles = weights()
        hidden = []
        for wgu, wd in tiles:
            tf = wd.shape[0]
            gu = _dot(xn, wgu)
            hidden.append((_silu(gu[:, :tf]) * gu[:, tf:]).astype(BF16))
        acc = _dot(jnp.concatenate(hidden, axis=1), jnp.concatenate([wd for _, wd in tiles], axis=0))
        acc = (x * (1.0 / FFN_RES) if first else o_ref[...]) + acc
        if last:
            y = FFN_RES * acc
            if epilogue == "final_norm":
                y = _rms_norm(y, egain_ref[...])
            elif epilogue == "norm_out":
                xn_out_ref[...] = _rms_norm(y, egain_ref[...]).astype(BF16)
            o_ref[...] = y
        else:
            o_ref[...] = acc

    pl.when(j == 0)(lambda: step(True, False))
    pl.when((j > 0) & (j < last_j))(lambda: step(False, False))
    pl.when(j == last_j)(lambda: step(False, True))


def _ffn_head_kernel(x_ref, gain_ref, wg_ref, wu_ref, wd_ref, egain_ref, *rest, epilogue, n_unused):
    rest = rest[n_unused:]
    if epilogue == "norm_out":
        o_ref, xn_out_ref, wgu16_ref, wd16_ref, xn_ref = rest
    else:
        o_ref, wgu16_ref, wd16_ref, xn_ref = rest
        xn_out_ref = None
    tf = wd_ref.shape[0]

    def weights():
        wgu16_ref[:, :tf] = wg_ref[...].astype(BF16)
        wgu16_ref[:, tf:] = wu_ref[...].astype(BF16)
        wd16_ref[...] = wd_ref[...].astype(BF16)
        return [(wgu16_ref[...], wd16_ref[...])]

    _ffn_step(pl.program_id(0), pl.num_programs(0) - 1, x_ref, gain_ref, egain_ref, o_ref, xn_out_ref, xn_ref,
              weights, epilogue)


def _ffn_tail_kernel(x_ref, gain_ref, wgu16_ref, wd16_ref, egain_ref, *rest, epilogue, n_unused, skip_first_tile):
    rest = rest[n_unused:]
    if epilogue == "norm_out":
        o_ref, xn_out_ref = rest
        xn_ref = xn_out_ref
    else:
        o_ref, xn_ref = rest
        xn_out_ref = None

    def steps():
        _ffn_step(pl.program_id(1), pl.num_programs(1) - 1, x_ref, gain_ref, egain_ref, o_ref, xn_out_ref, xn_ref,
                  lambda: [(wgu16_ref[s], wd16_ref[s]) for s in range(wd16_ref.shape[0])], epilogue)

    if skip_first_tile:
        i = pl.program_id(0)

        @pl.when((i == 0) & (pl.program_id(1) == 0))
        def _():
            o_ref[...] = jnp.zeros_like(o_ref)
            if xn_out_ref is not None:
                xn_out_ref[...] = jnp.zeros_like(xn_out_ref)

        pl.when(i > 0)(steps)
    else:
        steps()


def _ffn(x, gain, w_gate_up, w_down, layer, which, egain, *, epilogue, in_place=True, spare=None,
         tm=1024, tf=256, tail_tiles=2):
    t, d = x.shape
    f = w_down.shape[2]
    nf = f // tf
    assert nf % tail_tiles == 0 and nf // tail_tiles >= 2
    norm_out = epilogue == "norm_out"
    once = pl.Buffered(1)
    any_spec = pl.BlockSpec(memory_space=pl.ANY)
    n_act = 2 if norm_out else 1
    full_shapes = [jax.ShapeDtypeStruct((t, d), F32), jax.ShapeDtypeStruct((t, d), BF16)][:n_act]
    tile_shapes = [jax.ShapeDtypeStruct((tm, d), F32), jax.ShapeDtypeStruct((tm, d), BF16)][:n_act]
    norm_buf = [jnp.zeros((t, d), BF16) if spare is None else spare] if norm_out and in_place else []

    first_rows = pl.BlockSpec((tm, d), lambda j: (0, 0), pipeline_mode=once)
    head = pl.pallas_call(
        functools.partial(_ffn_head_kernel, epilogue=epilogue, n_unused=len(norm_buf)),
        grid=(nf,),
        in_specs=[
            first_rows,
            pl.BlockSpec((1, d), lambda j: (0, 0)),
            pl.BlockSpec((None, None, d, tf), lambda j: (layer, which, 0, j)),
            pl.BlockSpec((None, None, d, tf), lambda j: (layer, which, 0, j + nf)),
            pl.BlockSpec((None, None, tf, d), lambda j: (layer, which, j, 0)),
            pl.BlockSpec((1, d), lambda j: (0, 0)),
        ] + [any_spec] * len(norm_buf),
        out_specs=[first_rows] * n_act + [
            pl.BlockSpec((None, d, 2 * tf), lambda j: (j, 0, 0)),
            pl.BlockSpec((None, tf, d), lambda j: (j, 0, 0)),
        ],
        out_shape=(full_shapes if in_place else tile_shapes) + [jax.ShapeDtypeStruct((nf, d, 2 * tf), BF16),
                                                                jax.ShapeDtypeStruct((nf, tf, d), BF16)],
        input_output_aliases=({0: 0, 6: 1} if norm_out else {0: 0}) if in_place else {},
        scratch_shapes=[pltpu.VMEM((tm, d), BF16)],
        compiler_params=_params("arbitrary"),
        name="ffn_head",
    )(x, gain.reshape(1, d), w_gate_up, w_gate_up, w_down, egain.reshape(1, d), *norm_buf)
    *acts, wgu16, wd16 = head

    first = 1 if in_place else 0

    def rows(**kw):
        return pl.BlockSpec((tm, d), lambda i, j: (i + first, 0), **kw)

    n_steps = nf // tail_tiles
    x_lead = n_steps // 2

    def x_rows(**kw):
        return pl.BlockSpec(
            (tm, d), lambda i, j: (jnp.minimum(i + first + (j + x_lead) // n_steps, t // tm - 1), 0), **kw)

    def weight_tile(i, j):
        return (j if in_place else jnp.where(i == 0, 0, j), 0, 0)

    act = tm * d
    fixed_bytes = (act * 2 + 2 * act * 4
                   + 2 * tail_tiles * 3 * d * tf * 2
                   + tm * tf * (2 * 4 + 2))
    x_mode = {} if fixed_bytes + 2 * act * 4 <= VMEM_LIMIT_BYTES else {"pipeline_mode": once}

    tail = pl.pallas_call(
        functools.partial(_ffn_tail_kernel, epilogue=epilogue, n_unused=len(norm_buf), skip_first_tile=not in_place),
        grid=(t // tm - first, nf // tail_tiles),
        in_specs=[
            x_rows(**x_mode),
            pl.BlockSpec((1, d), lambda i, j: (0, 0)),
            pl.BlockSpec((tail_tiles, d, 2 * tf), weight_tile),
            pl.BlockSpec((tail_tiles, tf, d), weight_tile),
            pl.BlockSpec((1, d), lambda i, j: (0, 0)),
        ] + [any_spec] * len(norm_buf),
        out_specs=[rows()] + [rows(pipeline_mode=once)] * norm_out,
        out_shape=full_shapes,
        input_output_aliases=({0: 0, 5: 1} if norm_out else {0: 0}) if in_place else {},
        scratch_shapes=[] if norm_out else [pltpu.VMEM((tm, d), BF16)],
        compiler_params=_params("parallel", "arbitrary"),
        name="ffn_tail",
    )(acts[0] if in_place else x, gain.reshape(1, d), wgu16, wd16, egain.reshape(1, d),
      *(acts[1:] if in_place else []))
    if in_place:
        return tail
    return [lax.dynamic_update_slice(full, tile, (0, 0)) for full, tile in zip(tail, acts)]


def _out_proj_kernel(a_ref, w_ref, x_ref, o_ref, wb_ref):
    @pl.when(pl.program_id(0) == 0)
    def _():
        wb_ref[...] = w_ref[...].astype(BF16)

    o_ref[...] = x_ref[...] + _dot(a_ref[...], wb_ref[...])


def _out_proj(a, w, layer, x, *, tm=512):
    t, k = a.shape
    n = w.shape[2]
    return pl.pallas_call(
        _out_proj_kernel,
        grid=(t // tm,),
        in_specs=[
            pl.BlockSpec((tm, k), lambda i: (i, 0)),
            pl.BlockSpec((None, k, n), lambda i: (layer, 0, 0), pipeline_mode=pl.Buffered(1)),
            pl.BlockSpec((tm, n), lambda i: (i, 0)),
        ],
        out_specs=pl.BlockSpec((tm, n), lambda i: (i, 0)),
        out_shape=jax.ShapeDtypeStruct((t, n), F32),
        scratch_shapes=[pltpu.VMEM((k, n), BF16)],
        compiler_params=_params("arbitrary"),
        name="out_proj",
    )(a, w, x)


def _moba_rope_tables(seq):
    half = ROPE_DIM // 2
    f32 = np.float32
    inv_freq = np.power(f32(ROPE_THETA), -np.arange(half, dtype=f32) / f32(half))
    ang = np.arange(seq, dtype=f32)[:, None] * inv_freq[None, :]
    cos, sin = np.cos(ang), np.sin(ang)
    rest = MOBA_HEAD_DIM - ROPE_DIM
    c = np.concatenate([cos, cos, np.ones((seq, rest), f32)], axis=-1)
    a = np.concatenate([-sin, np.zeros((seq, half + rest), f32)], axis=-1)
    b = np.concatenate([np.zeros((seq, half), f32), sin, np.zeros((seq, rest), f32)], axis=-1)
    scale = f32(MOBA_HEAD_DIM ** -0.5) * np.log2(f32(np.e))
    return tuple(np.stack([t * scale, t]) for t in (c, a, b))


def _weight_half_specs(layer, d, tn, n_col, n_tok):
    blk = (None, d // 2, tn)
    top = pl.BlockSpec(blk, lambda j, i: (layer, 0, jnp.minimum(j + (i + 1) // n_tok, n_col - 1)))
    bottom = pl.BlockSpec(blk, lambda j, i: (layer, 1, j))
    return top, bottom


def _cast_weight_halves(w_top_ref, w_bottom_ref, wb_ref):
    half = w_top_ref.shape[0]
    wb_ref[:half, :] = w_top_ref[...].astype(BF16)
    wb_ref[half:, :] = w_bottom_ref[...].astype(BF16)


def _moba_proj_kernel(xn_ref, w_top_ref, w_bottom_ref, c_ref, a_ref, b_ref, o_ref, wb_ref, *, n_rot_tiles):
    j = pl.program_id(0)

    @pl.when(pl.program_id(1) == 0)
    def _():
        _cast_weight_halves(w_top_ref, w_bottom_ref, wb_ref)

    @pl.when(j < n_rot_tiles)
    def _():
        half = ROPE_DIM // 2
        for r in range(0, o_ref.shape[0], ROTARY_BAND_ROWS):
            band = slice(r, r + ROTARY_BAND_ROWS)
            y = _dot(xn_ref[band, :], wb_ref[...])
            c, a, b = c_ref[band, :], a_ref[band, :], b_ref[band, :]
            for h in range(y.shape[1] // MOBA_HEAD_DIM):
                sl = slice(h * MOBA_HEAD_DIM, (h + 1) * MOBA_HEAD_DIM)
                yh = y[:, sl]
                up = pltpu.roll(yh, MOBA_HEAD_DIM - half, 1)
                down = pltpu.roll(yh, half, 1)
                o_ref[band, sl] = (yh * c + up * a + down * b).astype(o_ref.dtype)

    @pl.when(j >= n_rot_tiles)
    def _():
        o_ref[...] = _dot(xn_ref[...], wb_ref[...]).astype(o_ref.dtype)


def _moba_proj(xn, w_qkv, layer, seq, *, tm=1024, tn=1024):
    t, d = xn.shape
    n = w_qkv.shape[2]
    tn = min(tn, d)
    c, a, b = _moba_rope_tables(seq)
    pos_tiles = seq // tm
    q_tiles = d // tn
    tab_spec = pl.BlockSpec((None, tm, MOBA_HEAD_DIM),
                            lambda j, i: (jnp.minimum(j // q_tiles, 1), i % pos_tiles, 0))
    return pl.pallas_call(
        functools.partial(_moba_proj_kernel, n_rot_tiles=2 * q_tiles),
        grid=(n // tn, t // tm),
        in_specs=[
            pl.BlockSpec((tm, d), lambda j, i: (i, 0)),
            *_weight_half_specs(layer, d, tn, n // tn, t // tm),
            tab_spec, tab_spec, tab_spec,
        ],
        out_specs=pl.BlockSpec((tm, tn), lambda j, i: (i, j)),
        out_shape=jax.ShapeDtypeStruct((t, n), BF16),
        scratch_shapes=[pltpu.VMEM((d, tn), BF16)],
        compiler_params=_params("arbitrary", "arbitrary"),
        name="moba_proj",
    )(xn, w_qkv, w_qkv, c, a, b)


def _split_bf16(x):
    hi = x.astype(BF16)
    lo = (x - hi.astype(F32)).astype(BF16)
    return hi, lo


def _moba_attn_kernel(q_ref, k_ref, v_ref, o_ref):
    seq = q_ref.shape[0]
    blk = MOBA_BLOCK
    n_blk = seq // blk
    heads = range(q_ref.shape[1] // MOBA_HEAD_DIM)
    cols = [slice(h * MOBA_HEAD_DIM, (h + 1) * MOBA_HEAD_DIM) for h in heads]

    k16 = [k_ref[:, c] for c in cols]
    ones = jnp.ones((BF16_SUBLANES, seq), BF16)
    vt_ext = [jnp.concatenate([v_ref[:, c].astype(F32).T.astype(BF16), ones], axis=0) for c in cols]
    km16 = []
    for k in k16:
        k_mean = jnp.mean(k.astype(F32).reshape(n_blk, blk, MOBA_HEAD_DIM), axis=1)
        km16.append(jnp.concatenate(_split_bf16(k_mean), axis=0))

    blk_row = lax.broadcasted_iota(jnp.int32, (n_blk, blk), 0)
    key_in_blk = lax.broadcasted_iota(jnp.int32, (blk, blk), 0)
    qry_in_blk = lax.broadcasted_iota(jnp.int32, (blk, blk), 1)
    causal = key_in_blk <= qry_in_blk

    def run(chains):
        st, gate, m, acc = {}, {}, {}, {}
        for ch in chains:
            h, i = ch
            q16 = q_ref[i * blk:(i + 1) * blk, cols[h]]
            st[ch] = _dot_nt(k16[h][:(i + 1) * blk], q16)
            if i > MOBA_TOPK:
                g2 = _dot_nt(km16[h], q16)
                gate[ch] = g2[:n_blk] + g2[n_blk:]
            m[ch] = jnp.full((1, blk), jnp.finfo(F32).min, F32)
            acc[ch] = None
        for j in range(max(i for _, i in chains) + 1):
            live = [ch for ch in chains if j <= ch[1]]
            p, m_new = {}, {}
            for ch in live:
                h, i = ch
                s = st[ch][j * blk:(j + 1) * blk]
                if j == i:
                    s = jnp.where(causal, s, -jnp.inf)
                col_max = jnp.max(s, axis=0, keepdims=True)
                if i > MOBA_TOPK and j < i:
                    gj = gate[ch][j:j + 1, :]
                    beats = ((gate[ch] > gj) | ((gate[ch] == gj) & (blk_row < j))) & (blk_row < i)
                    keep = jnp.sum(beats.astype(F32), axis=0, keepdims=True) < MOBA_TOPK
                    m_new[ch] = jnp.maximum(m[ch], jnp.where(keep, col_max, -jnp.inf))
                    shift = jnp.where(keep, m_new[ch], jnp.inf)
                else:
                    m_new[ch] = jnp.maximum(m[ch], col_max)
                    shift = m_new[ch]
                p[ch] = jnp.exp2(s - shift).astype(BF16)
            acc_j = {ch: _dot(vt_ext[ch[0]][:, j * blk:(j + 1) * blk], p[ch]) for ch in live}
            for ch in live:
                acc[ch] = acc_j[ch] if j == 0 else jnp.exp2(m[ch] - m_new[ch]) * acc[ch] + acc_j[ch]
                m[ch] = m_new[ch]
        for h, i in chains:
            out = acc[h, i][:MOBA_HEAD_DIM] / acc[h, i][MOBA_HEAD_DIM:MOBA_HEAD_DIM + 1]
            o_ref[i * blk:(i + 1) * blk, cols[h]] = out.T.astype(o_ref.dtype)

    for i in range(n_blk // 2):
        run([(h, ib) for h in heads for ib in (n_blk - 1 - i, i)])


def _moba_attn(qkv, batch, seq, d, *, heads_per_step=4):
    groups = d // (MOBA_HEAD_DIM * heads_per_step)
    blk = (seq, MOBA_HEAD_DIM * heads_per_step)
    return pl.pallas_call(
        _moba_attn_kernel,
        grid=(batch, groups),
        in_specs=[
            pl.BlockSpec(blk, lambda b, h: (b, h)),
            pl.BlockSpec(blk, lambda b, h: (b, groups + h)),
            pl.BlockSpec(blk, lambda b, h: (b, 2 * groups + h)),
        ],
        out_specs=pl.BlockSpec(blk, lambda b, h: (b, h)),
        out_shape=jax.ShapeDtypeStruct((batch * seq, d), BF16),
        compiler_params=_params("parallel", "parallel"),
        name="moba_attn",
    )(qkv, qkv, qkv)


def _ret_proj_kernel(xn_ref, w_top_ref, w_bottom_ref, cos_ref, sin_ref, o_ref, wb_ref, *, n_q_tiles):
    j = pl.program_id(0)
    half = RET_HEAD_DIM // 2

    @pl.when(pl.program_id(1) == 0)
    def _():
        _cast_weight_halves(w_top_ref, w_bottom_ref, wb_ref)

    def rotate(scale):
        y = _dot(xn_ref[...], wb_ref[...])
        cos, sin = cos_ref[...], sin_ref[...]
        for h in range(y.shape[1] // RET_HEAD_DIM):
            lo = h * RET_HEAD_DIM
            x1, x2 = y[:, lo:lo + half], y[:, lo + half:lo + RET_HEAD_DIM]
            o_ref[:, lo:lo + half] = ((x1 * cos - x2 * sin) * scale).astype(o_ref.dtype)
            o_ref[:, lo + half:lo + RET_HEAD_DIM] = ((x1 * sin + x2 * cos) * scale).astype(o_ref.dtype)

    pl.when(j < n_q_tiles)(lambda: rotate(1.0))
    pl.when((j >= n_q_tiles) & (j < 2 * n_q_tiles))(lambda: rotate(RET_HEAD_DIM ** -0.5))

    @pl.when(j >= 2 * n_q_tiles)
    def _():
        o_ref[...] = _dot(xn_ref[...], wb_ref[...]).astype(o_ref.dtype)


def _ret_proj(xn, w_in, layer, seq, *, tm=1024, tn=1024):
    t, d = xn.shape
    n = w_in.shape[2]
    tn = min(tn, d)
    half = RET_HEAD_DIM // 2
    inv_freq = np.power(np.float32(RET_ROT_BASE), -np.linspace(0.0, 1.0, half, dtype=np.float32))
    ang = np.arange(seq, dtype=np.float32)[:, None] * inv_freq[None, :]
    pos_tiles = seq // tm
    tab_spec = pl.BlockSpec((tm, half), lambda j, i: (i % pos_tiles, 0))
    return pl.pallas_call(
        functools.partial(_ret_proj_kernel, n_q_tiles=d // tn),
        grid=(n // tn, t // tm),
        in_specs=[
            pl.BlockSpec((tm, d), lambda j, i: (i, 0)),
            *_weight_half_specs(layer, d, tn, n // tn, t // tm),
            tab_spec, tab_spec,
        ],
        out_specs=pl.BlockSpec((tm, tn), lambda j, i: (i, j)),
        out_shape=jax.ShapeDtypeStruct((t, n), BF16),
        scratch_shapes=[pltpu.VMEM((d, tn), BF16)],
        compiler_params=_params("arbitrary", "arbitrary"),
        name="ret_proj",
    )(xn, w_in, w_in, np.cos(ang), np.sin(ang))


def _ret_kernel(q_ref, k_ref, v_ref, g_ref, gn_ref, dmask_ref, qdec_ref, kdec_ref, cdec_ref, wo_ref, x_ref,
                o_ref, state_ref, wb_ref):
    n = pl.program_id(1)

    @pl.when((pl.program_id(0) == 0) & (n == 0))
    def _():
        wb_ref[...] = wo_ref[...].astype(BF16)

    @pl.when(n == 0)
    def _():
        state_ref[...] = jnp.zeros_like(state_ref)

    heads = range(state_ref.shape[0])
    cols = [slice(h * RET_HEAD_DIM, (h + 1) * RET_HEAD_DIM) for h in heads]
    qc = [q_ref[:, c] for c in cols]
    kc = [k_ref[:, c] for c in cols]
    vc = [v_ref[:, c] for c in cols]
    s = [(_dot_nt(qc[h], kc[h]) * dmask_ref[h]).astype(BF16) for h in heads]
    cross = [_dot(qc[h], state_ref[h].astype(BF16)) * qdec_ref[h] for h in heads]
    out = [_dot(s[h], vc[h]) + cross[h] for h in heads]
    for h in heads:
        kt = (kc[h].astype(F32) * kdec_ref[h]).T.astype(BF16)
        state_ref[h] = state_ref[h] * cdec_ref[h] + _dot(kt, vc[h])
    for h in heads:
        o = out[h]
        o = o * lax.rsqrt(jnp.mean(o * o, axis=-1, keepdims=True) + RMS_EPS) * gn_ref[:, cols[h]]
        y = (_silu(g_ref[:, cols[h]].astype(F32)) * o).astype(BF16)
        o_ref[...] = (x_ref[...] if h == 0 else o_ref[...]) + _dot(y, wb_ref[cols[h], :])


def _retention(proj, gn_gain, w_o, layer, x, batch, seq):
    t, d = x.shape
    heads = d // RET_HEAD_DIM
    c = RET_CHUNK
    n_chunks = seq // c
    f32 = np.float32
    log_gamma = np.log1p(-np.power(f32(2.0), f32(-5.0) - np.arange(heads, dtype=f32)))
    n = np.arange(c, dtype=f32)
    diff = n[:, None] - n[None, :]
    dmask = np.exp(np.where((diff >= 0)[None], diff[None] * log_gamma[:, None, None], f32(-np.inf)))
    qdec = np.exp((n[None, :] + f32(1.0)) * log_gamma[:, None])[:, :, None]
    kdec = np.exp((f32(c - 1.0) - n[None, :]) * log_gamma[:, None])[:, :, None]
    cdec = np.ascontiguousarray(np.broadcast_to(np.exp(f32(c) * log_gamma)[:, None, None], (heads, 1, RET_HEAD_DIM)))

    def rows(col):
        return pl.BlockSpec((c, d), lambda b, n: (b * n_chunks + n, col))

    def whole(a):
        return pl.BlockSpec(a.shape, lambda b, n: (0,) * a.ndim)

    return pl.pallas_call(
        _ret_kernel,
        grid=(batch, n_chunks),
        in_specs=[
            rows(0), rows(1), rows(2), rows(3),
            pl.BlockSpec((None, 1, d), lambda b, n: (layer, 0, 0)),
            whole(dmask), whole(qdec), whole(kdec), whole(cdec),
            pl.BlockSpec((None, d, d), lambda b, n: (layer, 0, 0), pipeline_mode=pl.Buffered(1)),
            rows(0),
        ],
        out_specs=rows(0),
        out_shape=jax.ShapeDtypeStruct((t, d), F32),
        scratch_shapes=[pltpu.VMEM((heads, RET_HEAD_DIM, RET_HEAD_DIM), F32), pltpu.VMEM((d, d), BF16)],
        compiler_params=_params("arbitrary", "arbitrary"),
        name="retention",
    )(proj, proj, proj, proj, gn_gain.reshape(gn_gain.shape[0], 1, d), dmask, qdec, kdec, cdec, w_o, x)


def kernel(x, norm_gain, ffn_w_gate_up, ffn_w_down, moba_w_qkv, moba_w_o, ret_w_in, ret_w_o, ret_gn_gain, final_norm):
    batch, seq, d = x.shape
    depth = norm_gain.shape[0]
    h = x.reshape(batch * seq, d)
    hn = None
    for i in range(depth):
        g = norm_gain[i]
        h, hn = _ffn(h, g[0], ffn_w_gate_up, ffn_w_down, i, 0, g[1], epilogue="norm_out", in_place=i > 0, spare=hn)
        if i % 2 == 0:
            qkv = _moba_proj(hn, moba_w_qkv, i // 2, seq)
            h = _out_proj(_moba_attn(qkv, batch, seq, d), moba_w_o, i // 2, h)
        else:
            proj = _ret_proj(hn, ret_w_in, i // 2, seq)
            h = _retention(proj, ret_gn_gain, ret_w_o, i // 2, h, batch, seq)
        h = _ffn(h, g[2], ffn_w_gate_up, ffn_w_down, i, 1, final_norm,
                 epilogue="final_norm" if i == depth - 1 else "none")[0]
    return h.reshape(batch, seq, d)
```

```python
import functools

import jax
import jax.numpy as jnp
import numpy as np
from jax import lax
from jax.experimental import pallas as pl
from jax.experimental.pallas import tpu as pltpu

FFN_RES = 0.5
RMS_EPS = 1e-6

MOBA_HEAD_DIM = 128
MOBA_BLOCK = 256
MOBA_TOPK = 3
SCORE_GROUP_BLOCKS = 2
ROPE_THETA = 500000.0
ROPE_DIM = MOBA_HEAD_DIM // 4
ROTARY_BAND_ROWS = 256

RET_HEAD_DIM = 256
RET_CHUNK = 256
RET_ROT_BASE = 10000.0

V7X_VMEM_BYTES = 64 * 1024 * 1024
BF16_SUBLANES = 16
VMEM_LIMIT_BYTES = V7X_VMEM_BYTES - 8 * 1024 * 1024

F32 = jnp.float32
BF16 = jnp.bfloat16


def _params(*semantics):
    return pltpu.CompilerParams(dimension_semantics=semantics, vmem_limit_bytes=VMEM_LIMIT_BYTES)


def _rms_norm(x, gain):
    return x * lax.rsqrt(jnp.mean(x * x, axis=-1, keepdims=True) + RMS_EPS) * gain


def _dot(a, b):
    return jnp.dot(a, b, preferred_element_type=F32)


def _dot_nt(a, b):
    return lax.dot_general(a, b, (((1,), (1,)), ((), ())), preferred_element_type=F32)


def _silu(g):
    return g * jax.nn.sigmoid(g)


def _ffn_step(j, last_j, x_ref, gain_ref, egain_ref, o_ref, xn_out_ref, xn_ref, weights, epilogue):
    def step(first, last):
        if first:
            xn = _rms_norm(x_ref[...], gain_ref[...]).astype(BF16)
            xn_ref[...] = xn
        else:
            xn = xn_ref[...]
        tiles = weights()
        hidden = []
        for wgu, wd in tiles:
            tf = wd.shape[0]
            gu = _dot(xn, wgu)
            hidden.append((_silu(gu[:, :tf]) * gu[:, tf:]).astype(BF16))
        acc = _dot(jnp.concatenate(hidden, axis=1), jnp.concatenate([wd for _, wd in tiles], axis=0))
        if not first:
            acc = o_ref[...] + acc
        if last:
            y = x_ref[...] + FFN_RES * acc
            if epilogue == "final_norm":
                y = _rms_norm(y, egain_ref[...])
            elif epilogue == "norm_out":
                xn_out_ref[...] = _rms_norm(y, egain_ref[...]).astype(BF16)
            o_ref[...] = y
        else:
            o_ref[...] = acc

    pl.when(j == 0)(lambda: step(True, False))
    pl.when((j > 0) & (j < last_j))(lambda: step(False, False))
    pl.when(j == last_j)(lambda: step(False, True))


def _ffn_head_kernel(x_ref, gain_ref, wg_ref, wu_ref, wd_ref, egain_ref, *rest, epilogue, n_unused):
    rest = rest[n_unused:]
    if epilogue == "norm_out":
        o_ref, xn_out_ref, wgu16_ref, wd16_ref, xn_ref = rest
    else:
        o_ref, wgu16_ref, wd16_ref, xn_ref = rest
        xn_out_ref = None
    tf = wd_ref.shape[0]

    def weights():
        wgu16_ref[:, :tf] = wg_ref[...].astype(BF16)
        wgu16_ref[:, tf:] = wu_ref[...].astype(BF16)
        wd16_ref[...] = wd_ref[...].astype(BF16)
        return [(wgu16_ref[...], wd16_ref[...])]

    _ffn_step(pl.program_id(0), pl.num_programs(0) - 1, x_ref, gain_ref, egain_ref, o_ref, xn_out_ref, xn_ref,
              weights, epilogue)


def _ffn_tail_kernel(x_ref, gain_ref, wgu16_ref, wd16_ref, egain_ref, *rest, epilogue, n_unused, skip_first_tile):
    rest = rest[n_unused:]
    if epilogue == "norm_out":
        o_ref, xn_out_ref = rest
        xn_ref = xn_out_ref
    else:
        o_ref, xn_ref = rest
        xn_out_ref = None

    def steps():
        _ffn_step(pl.program_id(1), pl.num_programs(1) - 1, x_ref, gain_ref, egain_ref, o_ref, xn_out_ref, xn_ref,
                  lambda: [(wgu16_ref[s], wd16_ref[s]) for s in range(wd16_ref.shape[0])], epilogue)

    if skip_first_tile:
        i = pl.program_id(0)

        @pl.when((i == 0) & (pl.program_id(1) == 0))
        def _():
            o_ref[...] = jnp.zeros_like(o_ref)
            if xn_out_ref is not None:
                xn_out_ref[...] = jnp.zeros_like(xn_out_ref)

        pl.when(i > 0)(steps)
    else:
        steps()


def _ffn(x, gain, w_gate_up, w_down, layer, which, egain, *, epilogue, in_place=True, spare=None,
         tm=1024, tf=256, tail_tiles=2):
    t, d = x.shape
    f = w_down.shape[2]
    nf = f // tf
    assert nf % tail_tiles == 0 and nf // tail_tiles >= 2
    norm_out = epilogue == "norm_out"
    once = pl.Buffered(1)
    any_spec = pl.BlockSpec(memory_space=pl.ANY)
    n_act = 2 if norm_out else 1
    full_shapes = [jax.ShapeDtypeStruct((t, d), F32), jax.ShapeDtypeStruct((t, d), BF16)][:n_act]
    tile_shapes = [jax.ShapeDtypeStruct((tm, d), F32), jax.ShapeDtypeStruct((tm, d), BF16)][:n_act]
    norm_buf = [jnp.zeros((t, d), BF16) if spare is None else spare] if norm_out and in_place else []

    first_rows = pl.BlockSpec((tm, d), lambda j: (0, 0), pipeline_mode=once)
    head = pl.pallas_call(
        functools.partial(_ffn_head_kernel, epilogue=epilogue, n_unused=len(norm_buf)),
        grid=(nf,),
        in_specs=[
            first_rows,
            pl.BlockSpec((1, d), lambda j: (0, 0)),
            pl.BlockSpec((None, None, d, tf), lambda j: (layer, which, 0, j)),
            pl.BlockSpec((None, None, d, tf), lambda j: (layer, which, 0, j + nf)),
            pl.BlockSpec((None, None, tf, d), lambda j: (layer, which, j, 0)),
            pl.BlockSpec((1, d), lambda j: (0, 0)),
        ] + [any_spec] * len(norm_buf),
        out_specs=[first_rows] * n_act + [
            pl.BlockSpec((None, d, 2 * tf), lambda j: (j, 0, 0)),
            pl.BlockSpec((None, tf, d), lambda j: (j, 0, 0)),
        ],
        out_shape=(full_shapes if in_place else tile_shapes) + [jax.ShapeDtypeStruct((nf, d, 2 * tf), BF16),
                                                                jax.ShapeDtypeStruct((nf, tf, d), BF16)],
        input_output_aliases=({0: 0, 6: 1} if norm_out else {0: 0}) if in_place else {},
        scratch_shapes=[pltpu.VMEM((tm, d), BF16)],
        compiler_params=_params("arbitrary"),
        name="ffn_head",
    )(x, gain.reshape(1, d), w_gate_up, w_gate_up, w_down, egain.reshape(1, d), *norm_buf)
    *acts, wgu16, wd16 = head

    first = 1 if in_place else 0

    def rows(**kw):
        return pl.BlockSpec((tm, d), lambda i, j: (i + first, 0), **kw)

    def weight_tile(i, j):
        return (j if in_place else jnp.where(i == 0, 0, j), 0, 0)

    act = tm * d
    fixed_bytes = (act * 2 + 2 * act * 4
                   + 2 * tail_tiles * 3 * d * tf * 2
                   + tm * tf * (2 * 4 + 2))
    x_mode = {} if fixed_bytes + 2 * act * 4 <= VMEM_LIMIT_BYTES else {"pipeline_mode": once}

    tail = pl.pallas_call(
        functools.partial(_ffn_tail_kernel, epilogue=epilogue, n_unused=len(norm_buf), skip_first_tile=not in_place),
        grid=(t // tm - first, nf // tail_tiles),
        in_specs=[
            rows(**x_mode),
            pl.BlockSpec((1, d), lambda i, j: (0, 0)),
            pl.BlockSpec((tail_tiles, d, 2 * tf), weight_tile),
            pl.BlockSpec((tail_tiles, tf, d), weight_tile),
            pl.BlockSpec((1, d), lambda i, j: (0, 0)),
        ] + [any_spec] * len(norm_buf),
        out_specs=[rows()] + [rows(pipeline_mode=once)] * norm_out,
        out_shape=full_shapes,
        input_output_aliases=({0: 0, 5: 1} if norm_out else {0: 0}) if in_place else {},
        scratch_shapes=[] if norm_out else [pltpu.VMEM((tm, d), BF16)],
        compiler_params=_params("parallel", "arbitrary"),
        name="ffn_tail",
    )(acts[0] if in_place else x, gain.reshape(1, d), wgu16, wd16, egain.reshape(1, d),
      *(acts[1:] if in_place else []))
    if in_place:
        return tail
    return [lax.dynamic_update_slice(full, tile, (0, 0)) for full, tile in zip(tail, acts)]


def _out_proj_kernel(a_ref, w_ref, x_ref, o_ref, wb_ref):
    @pl.when(pl.program_id(0) == 0)
    def _():
        wb_ref[...] = w_ref[...].astype(BF16)

    o_ref[...] = x_ref[...] + _dot(a_ref[...], wb_ref[...])


def _out_proj(a, w, layer, x, *, tm=512):
    t, k = a.shape
    n = w.shape[2]
    return pl.pallas_call(
        _out_proj_kernel,
        grid=(t // tm,),
        in_specs=[
            pl.BlockSpec((tm, k), lambda i: (i, 0)),
            pl.BlockSpec((None, k, n), lambda i: (layer, 0, 0), pipeline_mode=pl.Buffered(1)),
            pl.BlockSpec((tm, n), lambda i: (i, 0)),
        ],
        out_specs=pl.BlockSpec((tm, n), lambda i: (i, 0)),
        out_shape=jax.ShapeDtypeStruct((t, n), F32),
        scratch_shapes=[pltpu.VMEM((k, n), BF16)],
        compiler_params=_params("arbitrary"),
        name="out_proj",
    )(a, w, x)


def _moba_rope_tables(seq):
    half = ROPE_DIM // 2
    f32 = np.float32
    inv_freq = np.power(f32(ROPE_THETA), -np.arange(half, dtype=f32) / f32(half))
    ang = np.arange(seq, dtype=f32)[:, None] * inv_freq[None, :]
    cos, sin = np.cos(ang), np.sin(ang)
    rest = MOBA_HEAD_DIM - ROPE_DIM
    c = np.concatenate([cos, cos, np.ones((seq, rest), f32)], axis=-1)
    a = np.concatenate([-sin, np.zeros((seq, half + rest), f32)], axis=-1)
    b = np.concatenate([np.zeros((seq, half), f32), sin, np.zeros((seq, rest), f32)], axis=-1)
    scale = f32(MOBA_HEAD_DIM ** -0.5) * np.log2(f32(np.e))
    return tuple(np.stack([t * scale, t]) for t in (c, a, b))


def _weight_half_specs(layer, d, tn, n_col, n_tok):
    blk = (None, d // 2, tn)
    top = pl.BlockSpec(blk, lambda j, i: (layer, 0, jnp.minimum(j + (i + 1) // n_tok, n_col - 1)))
    bottom = pl.BlockSpec(blk, lambda j, i: (layer, 1, j))
    return top, bottom


def _cast_weight_halves(w_top_ref, w_bottom_ref, wb_ref):
    half = w_top_ref.shape[0]
    wb_ref[:half, :] = w_top_ref[...].astype(BF16)
    wb_ref[half:, :] = w_bottom_ref[...].astype(BF16)


def _moba_proj_kernel(xn_ref, w_top_ref, w_bottom_ref, c_ref, a_ref, b_ref, o_ref, wb_ref, *, n_rot_tiles):
    j = pl.program_id(0)

    @pl.when(pl.program_id(1) == 0)
    def _():
        _cast_weight_halves(w_top_ref, w_bottom_ref, wb_ref)

    @pl.when(j < n_rot_tiles)
    def _():
        half = ROPE_DIM // 2
        for r in range(0, o_ref.shape[0], ROTARY_BAND_ROWS):
            band = slice(r, r + ROTARY_BAND_ROWS)
            y = _dot(xn_ref[band, :], wb_ref[...])
            c, a, b = c_ref[band, :], a_ref[band, :], b_ref[band, :]
            for h in range(y.shape[1] // MOBA_HEAD_DIM):
                sl = slice(h * MOBA_HEAD_DIM, (h + 1) * MOBA_HEAD_DIM)
                yh = y[:, sl]
                up = pltpu.roll(yh, MOBA_HEAD_DIM - half, 1)
                down = pltpu.roll(yh, half, 1)
                o_ref[band, sl] = (yh * c + up * a + down * b).astype(o_ref.dtype)

    @pl.when(j >= n_rot_tiles)
    def _():
        o_ref[...] = _dot(xn_ref[...], wb_ref[...]).astype(o_ref.dtype)


def _moba_proj(xn, w_qkv, layer, seq, *, tm=1024, tn=1024):
    t, d = xn.shape
    n = w_qkv.shape[2]
    tn = min(tn, d)
    c, a, b = _moba_rope_tables(seq)
    pos_tiles = seq // tm
    q_tiles = d // tn
    tab_spec = pl.BlockSpec((None, tm, MOBA_HEAD_DIM),
                            lambda j, i: (jnp.minimum(j // q_tiles, 1), i % pos_tiles, 0))
    return pl.pallas_call(
        functools.partial(_moba_proj_kernel, n_rot_tiles=2 * q_tiles),
        grid=(n // tn, t // tm),
        in_specs=[
            pl.BlockSpec((tm, d), lambda j, i: (i, 0)),
            *_weight_half_specs(layer, d, tn, n // tn, t // tm),
            tab_spec, tab_spec, tab_spec,
        ],
        out_specs=pl.BlockSpec((tm, tn), lambda j, i: (i, j)),
        out_shape=jax.ShapeDtypeStruct((t, n), BF16),
        scratch_shapes=[pltpu.VMEM((d, tn), BF16)],
        compiler_params=_params("arbitrary", "arbitrary"),
        name="moba_proj",
    )(xn, w_qkv, w_qkv, c, a, b)


def _split_bf16(x):
    hi = x.astype(BF16)
    lo = (x - hi.astype(F32)).astype(BF16)
    return hi, lo


def _moba_attn_kernel(q_ref, k_ref, v_ref, o_ref):
    seq = q_ref.shape[0]
    blk = MOBA_BLOCK
    n_blk = seq // blk
    heads = range(q_ref.shape[1] // MOBA_HEAD_DIM)
    cols = [slice(h * MOBA_HEAD_DIM, (h + 1) * MOBA_HEAD_DIM) for h in heads]

    k16 = [k_ref[:, c] for c in cols]
    ones = jnp.ones((BF16_SUBLANES, seq), BF16)
    vt_ext = [jnp.concatenate([v_ref[:, c].astype(F32).T.astype(BF16), ones], axis=0) for c in cols]
    km16 = []
    for k in k16:
        k_mean = jnp.mean(k.astype(F32).reshape(n_blk, blk, MOBA_HEAD_DIM), axis=1)
        km16.append(jnp.concatenate(_split_bf16(k_mean), axis=0))

    blk_row = lax.broadcasted_iota(jnp.int32, (n_blk, blk), 0)
    key_in_blk = lax.broadcasted_iota(jnp.int32, (blk, blk), 0)
    qry_in_blk = lax.broadcasted_iota(jnp.int32, (blk, blk), 1)
    causal = key_in_blk <= qry_in_blk

    def run(chains):
        q16, st, gate, m, acc = {}, {}, {}, {}, {}
        for ch in chains:
            h, i = ch
            q16[ch] = q_ref[i * blk:(i + 1) * blk, cols[h]]
            if i > MOBA_TOPK:
                g2 = _dot_nt(km16[h], q16[ch])
                gate[ch] = g2[:n_blk] + g2[n_blk:]
            m[ch] = jnp.full((1, blk), jnp.finfo(F32).min, F32)
            acc[ch] = None
        for j in range(max(i for _, i in chains) + 1):
            live = [ch for ch in chains if j <= ch[1]]
            p, m_new = {}, {}
            for ch in live:
                h, i = ch
                g0 = j - j % SCORE_GROUP_BLOCKS
                if j == g0:
                    st[ch] = _dot_nt(k16[h][g0 * blk:min(g0 + SCORE_GROUP_BLOCKS, i + 1) * blk], q16[ch])
                s = st[ch][(j - g0) * blk:(j - g0 + 1) * blk]
                if j == i:
                    s = jnp.where(causal, s, -jnp.inf)
                col_max = jnp.max(s, axis=0, keepdims=True)
                if i > MOBA_TOPK and j < i:
                    gj = gate[ch][j:j + 1, :]
                    beats = ((gate[ch] > gj) | ((gate[ch] == gj) & (blk_row < j))) & (blk_row < i)
                    keep = jnp.sum(beats.astype(F32), axis=0, keepdims=True) < MOBA_TOPK
                    m_new[ch] = jnp.maximum(m[ch], jnp.where(keep, col_max, -jnp.inf))
                    shift = jnp.where(keep, m_new[ch], jnp.inf)
                else:
                    m_new[ch] = jnp.maximum(m[ch], col_max)
                    shift = m_new[ch]
                p[ch] = jnp.exp2(s - shift).astype(BF16)
            acc_j = {ch: _dot(vt_ext[ch[0]][:, j * blk:(j + 1) * blk], p[ch]) for ch in live}
            for ch in live:
                acc[ch] = acc_j[ch] if j == 0 else jnp.exp2(m[ch] - m_new[ch]) * acc[ch] + acc_j[ch]
                m[ch] = m_new[ch]
        for h, i in chains:
            out = acc[h, i][:MOBA_HEAD_DIM] / acc[h, i][MOBA_HEAD_DIM:MOBA_HEAD_DIM + 1]
            o_ref[i * blk:(i + 1) * blk, cols[h]] = out.T.astype(o_ref.dtype)

    for i in range(n_blk // 2):
        run([(h, ib) for h in heads for ib in (n_blk - 1 - i, i)])


def _moba_attn(qkv, batch, seq, d, *, heads_per_step=4):
    groups = d // (MOBA_HEAD_DIM * heads_per_step)
    blk = (seq, MOBA_HEAD_DIM * heads_per_step)
    return pl.pallas_call(
        _moba_attn_kernel,
        grid=(batch, groups),
        in_specs=[
            pl.BlockSpec(blk, lambda b, h: (b, h)),
            pl.BlockSpec(blk, lambda b, h: (b, groups + h)),
            pl.BlockSpec(blk, lambda b, h: (b, 2 * groups + h)),
        ],
        out_specs=pl.BlockSpec(blk, lambda b, h: (b, h)),
        out_shape=jax.ShapeDtypeStruct((batch * seq, d), BF16),
        compiler_params=_params("parallel", "parallel"),
        name="moba_attn",
    )(qkv, qkv, qkv)


def _ret_proj_kernel(xn_ref, w_top_ref, w_bottom_ref, cos_ref, sin_ref, o_ref, wb_ref, *, n_q_tiles):
    j = pl.program_id(0)
    half = RET_HEAD_DIM // 2

    @pl.when(pl.program_id(1) == 0)
    def _():
        _cast_weight_halves(w_top_ref, w_bottom_ref, wb_ref)

    def rotate(scale):
        y = _dot(xn_ref[...], wb_ref[...])
        cos, sin = cos_ref[...], sin_ref[...]
        for h in range(y.shape[1] // RET_HEAD_DIM):
            lo = h * RET_HEAD_DIM
            x1, x2 = y[:, lo:lo + half], y[:, lo + half:lo + RET_HEAD_DIM]
            o_ref[:, lo:lo + half] = ((x1 * cos - x2 * sin) * scale).astype(o_ref.dtype)
            o_ref[:, lo + half:lo + RET_HEAD_DIM] = ((x1 * sin + x2 * cos) * scale).astype(o_ref.dtype)

    pl.when(j < n_q_tiles)(lambda: rotate(1.0))
    pl.when((j >= n_q_tiles) & (j < 2 * n_q_tiles))(lambda: rotate(RET_HEAD_DIM ** -0.5))

    @pl.when(j >= 2 * n_q_tiles)
    def _():
        o_ref[...] = _dot(xn_ref[...], wb_ref[...]).astype(o_ref.dtype)


def _ret_proj(xn, w_in, layer, seq, *, tm=1024, tn=1024):
    t, d = xn.shape
    n = w_in.shape[2]
    tn = min(tn, d)
    half = RET_HEAD_DIM // 2
    inv_freq = np.power(np.float32(RET_ROT_BASE), -np.linspace(0.0, 1.0, half, dtype=np.float32))
    ang = np.arange(seq, dtype=np.float32)[:, None] * inv_freq[None, :]
    pos_tiles = seq // tm
    tab_spec = pl.BlockSpec((tm, half), lambda j, i: (i % pos_tiles, 0))
    return pl.pallas_call(
        functools.partial(_ret_proj_kernel, n_q_tiles=d // tn),
        grid=(n // tn, t // tm),
        in_specs=[
            pl.BlockSpec((tm, d), lambda j, i: (i, 0)),
            *_weight_half_specs(layer, d, tn, n // tn, t // tm),
            tab_spec, tab_spec,
        ],
        out_specs=pl.BlockSpec((tm, tn), lambda j, i: (i, j)),
        out_shape=jax.ShapeDtypeStruct((t, n), BF16),
        scratch_shapes=[pltpu.VMEM((d, tn), BF16)],
        compiler_params=_params("arbitrary", "arbitrary"),
        name="ret_proj",
    )(xn, w_in, w_in, np.cos(ang), np.sin(ang))


def _ret_kernel(q_ref, k_ref, v_ref, g_ref, gn_ref, dmask_ref, qdec_ref, kdec_ref, cdec_ref, wo_ref, x_ref,
                o_ref, state_ref, wb_ref):
    n = pl.program_id(1)

    @pl.when((pl.program_id(0) == 0) & (n == 0))
    def _():
        wb_ref[...] = wo_ref[...].astype(BF16)

    @pl.when(n == 0)
    def _():
        state_ref[...] = jnp.zeros_like(state_ref)

    heads = range(state_ref.shape[0])
    cols = [slice(h * RET_HEAD_DIM, (h + 1) * RET_HEAD_DIM) for h in heads]
    qc = [q_ref[:, c] for c in cols]
    kc = [k_ref[:, c] for c in cols]
    vc = [v_ref[:, c] for c in cols]
    s = [(_dot_nt(qc[h], kc[h]) * dmask_ref[h]).astype(BF16) for h in heads]
    cross = [_dot(qc[h], state_ref[h].astype(BF16)) * qdec_ref[h] for h in heads]
    out = [_dot(s[h], vc[h]) + cross[h] for h in heads]
    for h in heads:
        kt = (kc[h].astype(F32) * kdec_ref[h]).T.astype(BF16)
        state_ref[h] = state_ref[h] * cdec_ref[h] + _dot(kt, vc[h])
    for h in heads:
        o = out[h]
        o = o * lax.rsqrt(jnp.mean(o * o, axis=-1, keepdims=True) + RMS_EPS) * gn_ref[:, cols[h]]
        y = (_silu(g_ref[:, cols[h]].astype(F32)) * o).astype(BF16)
        o_ref[...] = (x_ref[...] if h == 0 else o_ref[...]) + _dot(y, wb_ref[cols[h], :])


def _retention(proj, gn_gain, w_o, layer, x, batch, seq):
    t, d = x.shape
    heads = d // RET_HEAD_DIM
    c = RET_CHUNK
    n_chunks = seq // c
    f32 = np.float32
    log_gamma = np.log1p(-np.power(f32(2.0), f32(-5.0) - np.arange(heads, dtype=f32)))
    n = np.arange(c, dtype=f32)
    diff = n[:, None] - n[None, :]
    dmask = np.exp(np.where((diff >= 0)[None], diff[None] * log_gamma[:, None, None], f32(-np.inf)))
    qdec = np.exp((n[None, :] + f32(1.0)) * log_gamma[:, None])[:, :, None]
    kdec = np.exp((f32(c - 1.0) - n[None, :]) * log_gamma[:, None])[:, :, None]
    cdec = np.ascontiguousarray(np.broadcast_to(np.exp(f32(c) * log_gamma)[:, None, None], (heads, 1, RET_HEAD_DIM)))

    def rows(col):
        return pl.BlockSpec((c, d), lambda b, n: (b * n_chunks + n, col))

    def whole(a):
        return pl.BlockSpec(a.shape, lambda b, n: (0,) * a.ndim)

    return pl.pallas_call(
        _ret_kernel,
        grid=(batch, n_chunks),
        in_specs=[
            rows(0), rows(1), rows(2), rows(3),
            pl.BlockSpec((None, 1, d), lambda b, n: (layer, 0, 0)),
            whole(dmask), whole(qdec), whole(kdec), whole(cdec),
            pl.BlockSpec((None, d, d), lambda b, n: (layer, 0, 0), pipeline_mode=pl.Buffered(1)),
            rows(0),
        ],
        out_specs=rows(0),
        out_shape=jax.ShapeDtypeStruct((t, d), F32),
        scratch_shapes=[pltpu.VMEM((heads, RET_HEAD_DIM, RET_HEAD_DIM), F32), pltpu.VMEM((d, d), BF16)],
        compiler_params=_params("arbitrary", "arbitrary"),
        name="retention",
    )(proj, proj, proj, proj, gn_gain.reshape(gn_gain.shape[0], 1, d), dmask, qdec, kdec, cdec, w_o, x)


def kernel(x, norm_gain, ffn_w_gate_up, ffn_w_down, moba_w_qkv, moba_w_o, ret_w_in, ret_w_o, ret_gn_gain, final_norm):
    batch, seq, d = x.shape
    depth = norm_gain.shape[0]
    h = x.reshape(batch * seq, d)
    hn = None
    for i in range(depth):
        g = norm_gain[i]
        h, hn = _ffn(h, g[0], ffn_w_gate_up, ffn_w_down, i, 0, g[1], epilogue="norm_out", in_place=i > 0, spare=hn)
        if i % 2 == 0:
            qkv = _moba_proj(hn, moba_w_qkv, i // 2, seq)
            h = _out_proj(_moba_attn(qkv, batch, seq, d), moba_w_o, i // 2, h)
        else:
            proj = _ret_proj(hn, ret_w_in, i // 2, seq)
            h = _retention(proj, ret_gn_gain, ret_w_o, i // 2, h, batch, seq)
        h = _ffn(h, g[2], ffn_w_gate_up, ffn_w_down, i, 1, final_norm,
                 epilogue="final_norm" if i == depth - 1 else "none")[0]
    return h.reshape(batch, seq, d)
```
